```python
import math
import jax, jax.numpy as jnp
from jax import lax
import numpy as np

D_MODEL = 1024
BATCH = 8
SEQ = 8192
DEPTH = 2

N_MIXERS = 2
ROPE_THETA = 10000.0
LN_EPS = 1e-5
DN_ALPHA = (2 * DEPTH) ** 0.25
DN_BETA = (8 * DEPTH) ** -0.25

DA_HEAD_DIM = 64
DA_HEADS = D_MODEL // (2 * DA_HEAD_DIM)
DA_Q_BLOCK = 128

SW_HEAD_DIM = 64
SW_HEADS = D_MODEL // SW_HEAD_DIM
SW_KV_HEADS = max(1, SW_HEADS // 8)
SW_GROUP = SW_HEADS // SW_KV_HEADS
SW_WINDOW = 128
SW_BLOCK = 128

PK_HEADS = 8
PK_N_KEYS = 128
PK_N_EXPERTS = PK_N_KEYS * PK_N_KEYS
PK_KEY_DIM = 128
PK_TOPK = 16
PK_TOKEN_BLOCK = 128

N_DIFF_LAYERS = (DEPTH + 1) // 2
N_SWA_LAYERS = DEPTH // 2

kernel_name = 'hybrid_diffattn_swa_sinks_peer_deepnorm'


def layer_norm(x, g, b):
    xf = x.astype(jnp.float32)
    mu = xf.mean(-1, keepdims=True)
    var = jnp.square(xf - mu).mean(-1, keepdims=True)
    return ((xf - mu) * lax.rsqrt(var + LN_EPS) * g.astype(jnp.float32) + b.astype(jnp.float32)).astype(x.dtype)


def rms_norm(x, g):
    xf = x.astype(jnp.float32)
    y = xf * lax.rsqrt(jnp.mean(xf * xf, -1, keepdims=True) + LN_EPS) * g.astype(jnp.float32)
    return y.astype(x.dtype)


def rope_tables(seq, dim):
    inv = 1.0 / (ROPE_THETA ** (jnp.arange(0, dim, 2, dtype=jnp.float32) / dim))
    ang = jnp.arange(seq, dtype=jnp.float32)[:, None] * inv[None, :]
    return jnp.cos(ang), jnp.sin(ang)


def apply_rope(t, cos, sin):
    t1, t2 = jnp.split(t.astype(jnp.float32), 2, axis=-1)
    c = cos[None, :, None, :]
    s = sin[None, :, None, :]
    return jnp.concatenate([t1 * c - t2 * s, t2 * c + t1 * s], axis=-1).astype(t.dtype)


def diff_attention(x, w_qkv, lam_params, subln_g, w_o, lambda_init, cos, sin):
    B, S, _ = x.shape
    H, d = DA_HEADS, DA_HEAD_DIM
    q, k, v = jnp.split(x @ w_qkv, 3, axis=-1)
    q = apply_rope(q.reshape(B, S, 2 * H, d), cos, sin).reshape(B, S, H, 2, d)
    k = apply_rope(k.reshape(B, S, 2 * H, d), cos, sin).reshape(B, S, H, 2, d)
    v = v.reshape(B, S, H, 2 * d)
    lp = lam_params.astype(jnp.float32)
    lam = jnp.exp(jnp.sum(lp[0] * lp[1])) - jnp.exp(jnp.sum(lp[2] * lp[3])) + lambda_init
    nb = S // DA_Q_BLOCK
    qb = q.reshape(B, nb, DA_Q_BLOCK, H, 2, d).transpose(1, 0, 2, 3, 4, 5)
    kpos = jnp.arange(S)
    scale = d ** -0.5

    def one_block(args):
        qi, bi = args
        s = jnp.einsum('bqhpd,bkhpd->bhpqk', qi, k, preferred_element_type=jnp.float32) * scale
        qpos = bi * DA_Q_BLOCK + jnp.arange(DA_Q_BLOCK)
        s = jnp.where((kpos[None, :] <= qpos[:, None])[None, None, None], s, -jnp.inf)
        p = jax.nn.softmax(s, axis=-1)
        a = p[:, :, 0] - lam * p[:, :, 1]
        return jnp.einsum('bhqk,bkhe->bqhe', a.astype(v.dtype), v)

    o = lax.map(one_block, (qb, jnp.arange(nb)))
    o = o.transpose(1, 0, 2, 3, 4).reshape(B, S, H, 2 * d)
    o = rms_norm(o, subln_g) * (1.0 - lambda_init)
    return o.reshape(B, S, H * 2 * d) @ w_o


def swa_attention(x, w_qkv, b_qkv, sinks, w_o, b_o, cos, sin):
    B, S, _ = x.shape
    H, KV, G, d, L = SW_HEADS, SW_KV_HEADS, SW_GROUP, SW_HEAD_DIM, SW_BLOCK
    qkv = x @ w_qkv + b_qkv
    q, k, v = jnp.split(qkv, [H * d, (H + KV) * d], axis=-1)
    q = apply_rope(q.reshape(B, S, H, d), cos, sin)
    k = apply_rope(k.reshape(B, S, KV, d), cos, sin)
    v = v.reshape(B, S, KV, d)
    nb = S // L
    qb = q.reshape(B, nb, L, KV, G, d)

    def banded(t):
        tb = t.reshape(B, nb, L, KV, d)
        prev = jnp.pad(tb, ((0, 0), (1, 0), (0, 0), (0, 0), (0, 0)))[:, :-1]
        return jnp.concatenate([prev, tb], axis=2)

    kb, vb = banded(k), banded(v)
    s = jnp.einsum('bnqkgd,bnjkd->bnkgqj', qb, kb, preferred_element_type=jnp.float32) * (d ** -0.5)
    qi = jnp.arange(L)[:, None] + L
    kj = jnp.arange(2 * L)[None, :]
    rel = qi - kj
    local = (rel >= 0) & (rel < SW_WINDOW)
    valid = local[None] & ((jnp.arange(nb)[:, None, None] > 0) | (kj[None] >= L))
    s = jnp.where(valid[None, :, None, None], s, -jnp.inf)
    sink = sinks.astype(jnp.float32).reshape(KV, G)[None, None, :, :, None, None]
    m = jnp.maximum(s.max(-1, keepdims=True), sink)
    e = jnp.exp(s - m)
    p = e / (e.sum(-1, keepdims=True) + jnp.exp(sink - m))
    o = jnp.einsum('bnkgqj,bnjkd->bnqkgd', p.astype(vb.dtype), vb).reshape(B, S, H * d)
    return o @ w_o + b_o


def peer(x, w_query, sub_keys, u_emb, v_emb):
    B, S, D = x.shape
    K = PK_TOPK
    xt = x.reshape((B * S) // PK_TOKEN_BLOCK, PK_TOKEN_BLOCK, D)

    def one_block(xb):
        Tb = xb.shape[0]
        q = (xb @ w_query).reshape(Tb, PK_HEADS, 2, PK_KEY_DIM // 2)
        sc = jnp.einsum('thcd,hcnd->thcn', q, sub_keys, preferred_element_type=jnp.float32)
        s1, i1 = lax.top_k(sc[:, :, 0], K)
        s2, i2 = lax.top_k(sc[:, :, 1], K)
        cand = (s1[..., :, None] + s2[..., None, :]).reshape(Tb, PK_HEADS, K * K)
        cidx = (i1[..., :, None] * PK_N_KEYS + i2[..., None, :]).reshape(Tb, PK_HEADS, K * K)
        top_s, pos = lax.top_k(cand, K)
        eidx = jnp.take_along_axis(cidx, pos, axis=-1)
        g = jax.nn.softmax(top_s, axis=-1)
        u = u_emb[eidx]
        hval = jax.nn.gelu(jnp.einsum('thkd,td->thk', u, xb, preferred_element_type=jnp.float32), approximate=False)
        w = (g * hval).astype(xb.dtype)
        return jnp.einsum('thk,thkd->td', w, v_emb[eidx])

    return lax.map(one_block, xt).reshape(B, S, D)


def setup_inputs(seed: int = 0) -> dict:
    key = jax.random.key(seed)
    ks = jax.random.split(key, 20)
    f32 = jnp.float32
    D = D_MODEL

    def nrm(k, shape, scale):
        return jax.random.normal(k, shape, f32) * scale

    x = nrm(ks[0], (BATCH, SEQ, D), 1.0)
    da_w_qkv = nrm(ks[1], (N_DIFF_LAYERS, D, 3 * D), D ** -0.5)
    da_w_qkv = da_w_qkv.at[..., 2 * D:].multiply(DN_BETA)
    da_lambda = nrm(ks[2], (N_DIFF_LAYERS, 4, DA_HEAD_DIM), 0.1)
    da_subln_g = 1.0 + nrm(ks[3], (N_DIFF_LAYERS, 2 * DA_HEAD_DIM), 0.02)
    da_w_o = nrm(ks[4], (N_DIFF_LAYERS, D, D), D ** -0.5 * DN_BETA)
    sw_width = (SW_HEADS + 2 * SW_KV_HEADS) * SW_HEAD_DIM
    sw_w_qkv = nrm(ks[5], (N_SWA_LAYERS, D, sw_width), D ** -0.5)
    sw_w_qkv = sw_w_qkv.at[..., (SW_HEADS + SW_KV_HEADS) * SW_HEAD_DIM:].multiply(DN_BETA)
    sw_b_qkv = nrm(ks[6], (N_SWA_LAYERS, sw_width), 0.02)
    sw_sinks = nrm(ks[7], (N_SWA_LAYERS, SW_HEADS), 0.5)
    sw_w_o = nrm(ks[8], (N_SWA_LAYERS, SW_HEADS * SW_HEAD_DIM, D), (SW_HEADS * SW_HEAD_DIM) ** -0.5 * DN_BETA)
    sw_b_o = nrm(ks[9], (N_SWA_LAYERS, D), 0.02)
    pk_w_query = nrm(ks[10], (DEPTH, D, PK_HEADS * PK_KEY_DIM), D ** -0.5)
    pk_sub_keys = nrm(ks[11], (DEPTH, PK_HEADS, 2, PK_N_KEYS, PK_KEY_DIM // 2), (PK_KEY_DIM // 2) ** -0.5)
    pk_u = nrm(ks[12], (DEPTH, PK_N_EXPERTS, D), D ** -0.5)
    pk_v = nrm(ks[13], (DEPTH, PK_N_EXPERTS, D), DN_BETA * PK_HEADS ** -0.5)
    ln1_g = 1.0 + nrm(ks[14], (DEPTH, D), 0.02)
    ln1_b = nrm(ks[15], (DEPTH, D), 0.02)
    ln2_g = 1.0 + nrm(ks[16], (DEPTH, D), 0.02)
    ln2_b = nrm(ks[17], (DEPTH, D), 0.02)
    return {'x': x, 'da_w_qkv': da_w_qkv, 'da_lambda': da_lambda, 'da_subln_g': da_subln_g,
            'da_w_o': da_w_o, 'sw_w_qkv': sw_w_qkv, 'sw_b_qkv': sw_b_qkv, 'sw_sinks': sw_sinks,
            'sw_w_o': sw_w_o, 'sw_b_o': sw_b_o, 'pk_w_query': pk_w_query, 'pk_sub_keys': pk_sub_keys,
            'pk_u': pk_u, 'pk_v': pk_v, 'ln1_g': ln1_g, 'ln1_b': ln1_b, 'ln2_g': ln2_g, 'ln2_b': ln2_b}


def reference(x, da_w_qkv, da_lambda, da_subln_g, da_w_o, sw_w_qkv, sw_b_qkv, sw_sinks, sw_w_o, sw_b_o,
              pk_w_query, pk_sub_keys, pk_u, pk_v, ln1_g, ln1_b, ln2_g, ln2_b):
    S = x.shape[1]
    cos, sin = rope_tables(S, DA_HEAD_DIM)
    for i in range(DEPTH):
        j = i // N_MIXERS
        if i % N_MIXERS == 0:
            lambda_init = 0.8 - 0.6 * math.exp(-0.3 * i)
            mix = diff_attention(x, da_w_qkv[j], da_lambda[j], da_subln_g[j], da_w_o[j], lambda_init, cos, sin)
        else:
            mix = swa_attention(x, sw_w_qkv[j], sw_b_qkv[j], sw_sinks[j], sw_w_o[j], sw_b_o[j], cos, sin)
        x = layer_norm(DN_ALPHA * x + mix, ln1_g[i], ln1_b[i])
        x = layer_norm(DN_ALPHA * x + peer(x, pk_w_query[i], pk_sub_keys[i], pk_u[i], pk_v[i]), ln2_g[i], ln2_b[i])
    return x
```

```python
import functools
import math

import jax
import jax.numpy as jnp
from jax import lax
from jax.experimental import pallas as pl
from jax.experimental.pallas import tpu as pltpu

F32 = jnp.float32
BF16 = jnp.bfloat16

LANES = 128
HEAD_DIM = 64
ROPE_THETA = 10000.0
LN_EPS = 1e-5
NEG = -1e30
BIG = 1e30

PK_HEADS = 8
PK_KEYS = 128
PK_TOPK = 16
SW_WINDOW = 128
VMEM_LIMIT = 48 * 1024 * 1024

_NT = (((1,), (1,)), ((), ()))


def _cparams(n_axes):
    return pltpu.CompilerParams(dimension_semantics=("arbitrary",) * n_axes,
                                vmem_limit_bytes=VMEM_LIMIT)


def _qkv_kernel(x_ref, w_ref, b_ref, c_ref, s1_ref, s2_ref, o_ref, *,
                n_groups, n_q_groups, n_rope_groups, chunk, q_scale):
    xb = x_ref[...].astype(BF16)
    cos = c_ref[...]
    sin_lo = s1_ref[...]
    sin_hi = s2_ref[...]
    for c0 in range(0, n_groups, chunk):
        ng = min(chunk, n_groups - c0)
        cols = slice(c0 * LANES, (c0 + ng) * LANES)
        acc = jnp.dot(xb, w_ref[:, cols], preferred_element_type=F32) + b_ref[:, cols]
        for g in range(ng):
            gg = c0 + g
            a = acc[:, g * LANES:(g + 1) * LANES]
            if gg < n_rope_groups:
                a = a * cos + pltpu.roll(a, 96, 1) * sin_lo + pltpu.roll(a, 32, 1) * sin_hi
            if gg < n_q_groups:
                a = a * q_scale
            o_ref[:, gg * LANES:(gg + 1) * LANES] = a.astype(BF16)


def _qkv_rope(x2d, w, bias, tables, seq, n_q_groups, n_rope_groups, tm=512):
    T, D = x2d.shape
    N = w.shape[1]
    n_groups = N // LANES
    pos_blocks = seq // tm
    kern = functools.partial(_qkv_kernel, n_groups=n_groups, n_q_groups=n_q_groups,
                             n_rope_groups=n_rope_groups, chunk=4,
                             q_scale=HEAD_DIM ** -0.5)
    tab_spec = pl.BlockSpec((tm, LANES), lambda i: (i % pos_blocks, 0))
    return pl.pallas_call(
        kern,
        grid=(T // tm,),
        in_specs=[pl.BlockSpec((tm, D), lambda i: (i, 0)),
                  pl.BlockSpec((D, N), lambda i: (0, 0)),
                  pl.BlockSpec((1, N), lambda i: (0, 0)),
                  tab_spec, tab_spec, tab_spec],
        out_specs=pl.BlockSpec((tm, N), lambda i: (i, 0)),
        out_shape=jax.ShapeDtypeStruct((T, N), BF16),
        compiler_params=_cparams(1),
        name="qkv_rope",
    )(x2d, w, bias, *tables)


def _diff_attn_kernel(lam_ref, g_ref, q_ref, k_ref, vt_ref, o_ref, *, tq, tk, lambda_init):
    qi = pl.program_id(2)
    q = q_ref[0]
    lane = lax.broadcasted_iota(jnp.int32, (tq, LANES), 1)
    zero = jnp.zeros_like(q)
    qs = jnp.concatenate([jnp.where(lane < HEAD_DIM, q, zero),
                          jnp.where(lane >= HEAD_DIM, q, zero)], axis=0)

    def step(j, carry, masked):
        m, l, acc = carry
        kb = k_ref[0, pl.ds(pl.multiple_of(j * tk, tk), tk), :]
        vt = vt_ref[0, 0, j]
        s = lax.dot_general(kb, qs, _NT, preferred_element_type=F32)
        if masked:
            key = j * tk + lax.broadcasted_iota(jnp.int32, (tk, 2 * tq), 0)
            col = lax.broadcasted_iota(jnp.int32, (tk, 2 * tq), 1)
            qpos = qi * tq + jnp.where(col >= tq, col - tq, col)
            s = jnp.where(key <= qpos, s, NEG)
        m_new = jnp.maximum(m, jnp.max(s, axis=0, keepdims=True))
        alpha = jnp.exp(m - m_new)
        p = jnp.exp(s - m_new)
        l = alpha * l + jnp.sum(p, axis=0, keepdims=True)
        acc = alpha * acc + jnp.dot(vt, p.astype(BF16), preferred_element_type=F32)
        return m_new, l, acc

    init = (jnp.full((1, 2 * tq), NEG, F32), jnp.zeros((1, 2 * tq), F32),
            jnp.zeros((LANES, 2 * tq), F32))
    n_full = (qi * tq) // tk
    carry = lax.fori_loop(0, n_full, lambda j, c: step(j, c, False), init)
    m, l, acc = step(n_full, carry, True)

    lp = lam_ref[...]
    lam = (jnp.exp(jnp.sum(lp[0:1] * lp[1:2], axis=1, keepdims=True))
           - jnp.exp(jnp.sum(lp[2:3] * lp[3:4], axis=1, keepdims=True)) + lambda_init)
    o = acc / l
    od = o[:, :tq] - lam * o[:, tq:]
    ms = jnp.mean(od * od, axis=0, keepdims=True)
    y = od * lax.rsqrt(ms + LN_EPS) * g_ref[...] * (1.0 - lambda_init)
    o_ref[0] = y.T.astype(BF16)


def _diff_attention(qkv, lam_params, subln_g, batch, seq, n_heads, lambda_init, tq=256, tk=512):
    T, N = qkv.shape
    d_model = n_heads * LANES
    qkv3 = qkv.reshape(batch, seq, N)
    v = qkv3[:, :, 2 * d_model:]
    vt = jnp.transpose(v.reshape(batch, seq // tk, tk, n_heads, LANES), (0, 3, 1, 4, 2))
    kern = functools.partial(_diff_attn_kernel, tq=tq, tk=tk, lambda_init=lambda_init)
    out = pl.pallas_call(
        kern,
        grid=(batch, n_heads, seq // tq),
        in_specs=[pl.BlockSpec((4, HEAD_DIM), lambda b, h, i: (0, 0)),
                  pl.BlockSpec((LANES, 1), lambda b, h, i: (0, 0)),
                  pl.BlockSpec((1, tq, LANES), lambda b, h, i: (b, i, h)),
                  pl.BlockSpec((1, seq, LANES), lambda b, h, i: (b, 0, n_heads + h)),
                  pl.BlockSpec((1, 1, seq // tk, LANES, tk), lambda b, h, i: (b, h, 0, 0, 0))],
        out_specs=pl.BlockSpec((1, tq, LANES), lambda b, h, i: (b, i, h)),
        out_shape=jax.ShapeDtypeStruct((batch, seq, d_model), BF16),
        compiler_params=_cparams(3),
        name="diff_attention",
    )(lam_params.astype(F32), subln_g.astype(F32).reshape(LANES, 1), qkv3, qkv3, vt)
    return out.reshape(T, d_model)


def _swa_kernel(sink_ref, q_ref, kc_ref, kp_ref, ksc_ref, ksp_ref,
                vc_ref, vp_ref, vsc_ref, vsp_ref, o_ref, *, blk, n_kv, group):
    n = pl.program_id(1)
    lane = lax.broadcasted_iota(jnp.int32, (blk, LANES), 1)
    lo = lane < HEAD_DIM
    k_nat = jnp.concatenate([kp_ref[0], kc_ref[0]], axis=0)
    k_swp = jnp.concatenate([ksp_ref[0], ksc_ref[0]], axis=0)
    v_nat = jnp.concatenate([vp_ref[0], vc_ref[0]], axis=0)
    v_swp = jnp.concatenate([vsp_ref[0], vsc_ref[0]], axis=0)
    qrow = lax.broadcasted_iota(jnp.int32, (blk, 2 * blk), 0)
    kcol = lax.broadcasted_iota(jnp.int32, (blk, 2 * blk), 1)
    rel = qrow + blk - kcol
    first_key = jnp.where(n > 0, 0, blk)
    valid = (rel >= 0) & (rel < SW_WINDOW) & (kcol >= first_key)
    pairs = group // 2
    for kv in range(n_kv):
        o_par = []
        for parity in range(2):
            use_nat = (kv == 0) == (parity == 0)
            kx = k_nat if use_nat else k_swp
            vx = v_nat if use_nat else v_swp
            qm = []
            for p in range(pairs):
                gcol = (kv * pairs + p) * LANES
                qg = q_ref[0, :, gcol:gcol + LANES]
                keep = lo if parity == 0 else ~lo
                qm.append(jnp.where(keep, qg, jnp.zeros_like(qg)))
            qst = jnp.concatenate(qm, axis=0)
            s_all = lax.dot_general(qst, kx, _NT, preferred_element_type=F32)
            ps = []
            for p in range(pairs):
                head = kv * group + 2 * p + parity
                s = jnp.where(valid, s_all[p * blk:(p + 1) * blk], NEG)
                sk = sink_ref[head:head + 1, 0:1]
                m = jnp.maximum(jnp.max(s, axis=-1, keepdims=True), sk)
                e = jnp.exp(s - m)
                den = jnp.sum(e, axis=-1, keepdims=True) + jnp.exp(sk - m)
                ps.append((e / den).astype(BF16))
            o_par.append(jnp.dot(jnp.concatenate(ps, axis=0), vx, preferred_element_type=F32))
        for p in range(pairs):
            gcol = (kv * pairs + p) * LANES
            rows = slice(p * blk, (p + 1) * blk)
            o_ref[0, :, gcol:gcol + LANES] = jnp.where(lo, o_par[0][rows], o_par[1][rows]).astype(BF16)


def _swa_attention(qkv, sinks, batch, seq, n_heads, n_kv, blk=128):
    T, N = qkv.shape
    d_model = n_heads * HEAD_DIM
    qg = d_model // LANES
    qkv3 = qkv.reshape(batch, seq, N)
    nb = seq // blk
    sink_b = jnp.broadcast_to(sinks.astype(F32)[:, None], (n_heads, LANES))

    def cur(col):
        return pl.BlockSpec((1, blk, LANES), lambda b, n: (b, n, col))

    def prev(col):
        return pl.BlockSpec((1, blk, LANES), lambda b, n: (b, jnp.maximum(n - 1, 0), col))

    kern = functools.partial(_swa_kernel, blk=blk, n_kv=n_kv, group=n_heads // n_kv)
    out = pl.pallas_call(
        kern,
        grid=(batch, nb),
        in_specs=[pl.BlockSpec((n_heads, LANES), lambda b, n: (0, 0)),
                  pl.BlockSpec((1, blk, d_model), lambda b, n: (b, n, 0)),
                  cur(qg), prev(qg), cur(qg + 1), prev(qg + 1),
                  cur(qg + 2), prev(qg + 2), cur(qg + 3), prev(qg + 3)],
        out_specs=pl.BlockSpec((1, blk, d_model), lambda b, n: (b, n, 0)),
        out_shape=jax.ShapeDtypeStruct((batch, seq, d_model), BF16),
        compiler_params=_cparams(2),
        name="swa_attention",
    )(sink_b, qkv3, qkv3, qkv3, qkv3, qkv3, qkv3, qkv3, qkv3, qkv3)
    return out.reshape(T, d_model)


def _layer_norm(y, g, b):
    mu = jnp.mean(y, axis=-1, keepdims=True)
    yc = y - mu
    var = jnp.mean(yc * yc, axis=-1, keepdims=True)
    return yc * lax.rsqrt(var + LN_EPS) * g + b


def _proj_ln_kernel(x_ref, o_ref, w_ref, b_ref, g_ref, beta_ref, out_ref, *, alpha):
    mix = jnp.dot(o_ref[...], w_ref[...], preferred_element_type=F32) + b_ref[...]
    out_ref[...] = _layer_norm(alpha * x_ref[...] + mix, g_ref[...], beta_ref[...])


def _proj_ln(x2d, o2d, w, bias, g, beta, alpha, tm=512):
    T, D = x2d.shape
    K = o2d.shape[1]
    row = lambda i: (i, 0)
    fixed = lambda i: (0, 0)
    return pl.pallas_call(
        functools.partial(_proj_ln_kernel, alpha=alpha),
        grid=(T // tm,),
        in_specs=[pl.BlockSpec((tm, D), row), pl.BlockSpec((tm, K), row),
                  pl.BlockSpec((K, D), fixed), pl.BlockSpec((1, D), fixed),
                  pl.BlockSpec((1, D), fixed), pl.BlockSpec((1, D), fixed)],
        out_specs=pl.BlockSpec((tm, D), row),
        out_shape=jax.ShapeDtypeStruct((T, D), F32),
        compiler_params=_cparams(1),
        name="proj_ln",
    )(x2d, o2d, w, bias, g, beta)


def _tree(fn, xs):
    xs = list(xs)
    while len(xs) > 1:
        nxt = [fn(xs[i], xs[i + 1]) for i in range(0, len(xs) - 1, 2)]
        if len(xs) % 2:
            nxt.append(xs[-1])
        xs = nxt
    return xs[0]


def _peer_route_kernel(x_ref, wq_ref, kp_ref, s2_o, b_o, a_o, tau_o,
                       q_scr, sc_scr, top_scr, st_scr):
    q = jnp.dot(x_ref[...].astype(BF16), wq_ref[...], preferred_element_type=F32)
    for h in range(PK_HEADS):
        q_scr[h] = q[:, h * LANES:(h + 1) * LANES].astype(BF16)

    def scores_and_tops(h, carry):
        qh = q_scr[h]
        for c in range(2):
            sc = lax.dot_general(kp_ref[h, c], qh, _NT, preferred_element_type=F32)
            sc_scr[h, c] = sc
            cur = sc
            for k in range(PK_TOPK):
                m = jnp.max(cur, axis=0, keepdims=True)
                top_scr[c, k, pl.ds(h, 1), :] = m
                if k + 1 < PK_TOPK:
                    cur = jnp.where(cur == m, NEG, cur)
        return carry

    lax.fori_loop(0, PK_HEADS, scores_and_tops, 0)

    t1 = [top_scr[0, k] for k in range(PK_TOPK)]
    t2 = [top_scr[1, k] for k in range(PK_TOPK)]
    cands = [t1[r] + t2[j] for r in range(PK_TOPK) for j in range(PK_TOPK)
             if (r + 1) * (j + 1) <= PK_TOPK]
    cur = cands
    thr = None
    for k in range(PK_TOPK):
        thr = _tree(jnp.maximum, cur)
        if k + 1 < PK_TOPK:
            cur = [jnp.where(c == thr, NEG, c) for c in cur]
    cmax = cands[0]
    z = _tree(jnp.add, [jnp.where(c >= thr, jnp.exp(c - cmax), 0.0) for c in cands])
    st_scr[0] = thr
    st_scr[1] = 0.5 / z

    def per_head(h, carry):
        s1 = sc_scr[h, 0]
        s2 = sc_scr[h, 1]
        thr_h = st_scr[0, pl.ds(h, 1), :]
        tau = jnp.full(s1.shape, BIG, F32)
        for k in range(PK_TOPK):
            t2k = top_scr[1, k, pl.ds(h, 1), :]
            tau = jnp.where(s1 + t2k >= thr_h, t2k, tau)
        s2_o[h] = s2
        tau_o[h] = tau
        a_o[h] = jnp.exp(s1 - top_scr[0, 0, pl.ds(h, 1), :])
        b_o[h] = jnp.exp(s2 - top_scr[1, 0, pl.ds(h, 1), :]) * st_scr[1, pl.ds(h, 1), :]
        return carry

    lax.fori_loop(0, PK_HEADS, per_head, 0)


def _peer_route(x2d, wq, keys_padded, tm=256):
    T, D = x2d.shape
    out = jax.ShapeDtypeStruct((PK_HEADS, PK_KEYS, T), F32)
    ospec = pl.BlockSpec((PK_HEADS, PK_KEYS, tm), lambda i: (0, 0, i))
    return pl.pallas_call(
        _peer_route_kernel,
        grid=(T // tm,),
        in_specs=[pl.BlockSpec((tm, D), lambda i: (i, 0)),
                  pl.BlockSpec(wq.shape, lambda i: (0, 0)),
                  pl.BlockSpec(keys_padded.shape, lambda i: (0, 0, 0, 0))],
        out_specs=[ospec, ospec, ospec, ospec],
        out_shape=[out, out, out, out],
        scratch_shapes=[pltpu.VMEM((PK_HEADS, tm, LANES), BF16),
                        pltpu.VMEM((PK_HEADS, 2, PK_KEYS, tm), F32),
                        pltpu.VMEM((2, PK_TOPK, PK_HEADS, tm), F32),
                        pltpu.VMEM((2, PK_HEADS, tm), F32)],
        compiler_params=_cparams(1),
        name="peer_route",
    )(x2d, wq, keys_padded)


def _peer_dense_kernel(x_ref, u_ref, vt_ref, s2_ref, b_ref, a_ref, tau_ref, g_ref, beta_ref,
                       out_ref, xb_scr, h_scr, w_scr, acc_scr, *, alpha, rt):
    j = pl.program_id(1)
    nj = pl.num_programs(1)

    @pl.when(j == 0)
    def _():
        xb_scr[...] = x_ref[...].astype(BF16)
        acc_scr[...] = jnp.zeros_like(acc_scr)

    h_scr[...] = lax.dot_general(u_ref[...], xb_scr[...], _NT, preferred_element_type=F32)
    tiles_per_key = PK_KEYS // rt

    def tile(it, carry):
        i1l = it // tiles_per_key
        r2 = pl.multiple_of((it % tiles_per_key) * rt, rt)
        rows = pl.ds(pl.multiple_of(it * rt, rt), rt)
        hh = h_scr[rows, :]
        gate = jnp.zeros_like(hh)
        for h in range(PK_HEADS):
            sel = s2_ref[h, pl.ds(r2, rt), :] >= tau_ref[h, pl.ds(i1l, 1), :]
            gate = gate + jnp.where(sel, b_ref[h, pl.ds(r2, rt), :], 0.0) * a_ref[h, pl.ds(i1l, 1), :]
        w = gate * hh * (1.0 + lax.erf(hh * (2.0 ** -0.5)))
        w_scr[rows, :] = w.astype(BF16)
        return carry

    lax.fori_loop(0, h_scr.shape[0] // rt, tile, 0)
    acc_scr[...] += jnp.dot(vt_ref[...], w_scr[...], preferred_element_type=F32)

    @pl.when(j == nj - 1)
    def _():
        y = alpha * x_ref[...] + acc_scr[...].T
        out_ref[...] = _layer_norm(y, g_ref[...], beta_ref[...])


def _peer_dense(x2d, u, vt, s2, b, a, tau, g, beta, alpha, tt=512, ec=1024, rt=32):
    T, D = x2d.shape
    n_exp = u.shape[0]
    keys_per_chunk = ec // PK_KEYS
    tok = pl.BlockSpec((PK_HEADS, PK_KEYS, tt), lambda i, j: (0, 0, i))
    key = pl.BlockSpec((PK_HEADS, keys_per_chunk, tt), lambda i, j: (0, j, i))
    fixed = pl.BlockSpec((1, D), lambda i, j: (0, 0))
    return pl.pallas_call(
        functools.partial(_peer_dense_kernel, alpha=alpha, rt=rt),
        grid=(T // tt, n_exp // ec),
        in_specs=[pl.BlockSpec((tt, D), lambda i, j: (i, 0)),
                  pl.BlockSpec((ec, D), lambda i, j: (j, 0)),
                  pl.BlockSpec((D, ec), lambda i, j: (0, j)),
                  tok, tok, key, key, fixed, fixed],
        out_specs=pl.BlockSpec((tt, D), lambda i, j: (i, 0)),
        out_shape=jax.ShapeDtypeStruct((T, D), F32),
        scratch_shapes=[pltpu.VMEM((tt, D), BF16),
                        pltpu.VMEM((ec, tt), F32),
                        pltpu.VMEM((ec, tt), BF16),
                        pltpu.VMEM((D, tt), F32)],
        compiler_params=_cparams(2),
        name="peer_dense",
    )(x2d, u, vt, s2, b, a, tau, g, beta)


def _rope_tables(seq):
    half = HEAD_DIM // 2
    inv = 1.0 / (ROPE_THETA ** (jnp.arange(0, HEAD_DIM, 2, dtype=F32) / HEAD_DIM))
    ang = jnp.arange(seq, dtype=F32)[:, None] * inv[None, :]
    cos = jnp.tile(jnp.cos(ang), (1, LANES // half))
    sin = jnp.tile(jnp.sin(ang), (1, LANES // half))
    first_half = (jnp.arange(LANES) % HEAD_DIM) < half
    return (cos, jnp.where(first_half[None, :], -sin, 0.0), jnp.where(first_half[None, :], 0.0, sin))


def _swap_halves(w):
    n = w.shape[-1] // 2
    return jnp.concatenate([w[..., n:], w[..., :n]], axis=-1)


def kernel(x, da_w_qkv, da_lambda, da_subln_g, da_w_o, sw_w_qkv, sw_b_qkv, sw_sinks, sw_w_o, sw_b_o,
           pk_w_query, pk_sub_keys, pk_u, pk_v, ln1_g, ln1_b, ln2_g, ln2_b):
    batch, seq, d_model = x.shape
    depth = pk_w_query.shape[0]
    T = batch * seq
    alpha = (2 * depth) ** 0.25
    tables = _rope_tables(seq)
    da_heads = d_model // (2 * HEAD_DIM)
    sw_heads = d_model // HEAD_DIM
    sw_kv = (sw_w_qkv.shape[-1] - d_model) // (2 * HEAD_DIM)
    assert sw_kv == 2 and sw_kv * HEAD_DIM == LANES
    row = lambda v: v.astype(F32).reshape(1, -1)

    x2d = x.reshape(T, d_model).astype(F32)
    for i in range(depth):
        j = i // 2
        if i % 2 == 0:
            lambda_init = 0.8 - 0.6 * math.exp(-0.3 * i)
            w = da_w_qkv[j].astype(BF16)
            qkv = _qkv_rope(x2d, w, jnp.zeros((1, w.shape[1]), F32), tables, seq,
                            n_q_groups=da_heads, n_rope_groups=2 * da_heads)
            mix_in = _diff_attention(qkv, da_lambda[j], da_subln_g[j], batch, seq, da_heads, lambda_init)
            w_o, b_o = da_w_o[j].astype(BF16), jnp.zeros((1, d_model), F32)
        else:
            wq_, wk_, wv_ = (sw_w_qkv[j][:, :d_model], sw_w_qkv[j][:, d_model:d_model + LANES],
                             sw_w_qkv[j][:, d_model + LANES:])
            bq_, bk_, bv_ = (sw_b_qkv[j][:d_model], sw_b_qkv[j][d_model:d_model + LANES],
                             sw_b_qkv[j][d_model + LANES:])
            w = jnp.concatenate([wq_, wk_, _swap_halves(wk_), wv_, _swap_halves(wv_)], axis=1).astype(BF16)
            bias = jnp.concatenate([bq_, bk_, _swap_halves(bk_), bv_, _swap_halves(bv_)]).astype(F32)
            n_qg = d_model // LANES
            qkv = _qkv_rope(x2d, w, bias.reshape(1, -1), tables, seq,
                            n_q_groups=n_qg, n_rope_groups=n_qg + 2)
            mix_in = _swa_attention(qkv, sw_sinks[j], batch, seq, sw_heads, sw_kv)
            w_o, b_o = sw_w_o[j].astype(BF16), row(sw_b_o[j])
        x2d = _proj_ln(x2d, mix_in, w_o, b_o, row(ln1_g[i]), row(ln1_b[i]), alpha)

        keys = pk_sub_keys[i].astype(BF16)
        zk = jnp.zeros_like(keys[:, 0])
        keys_padded = jnp.stack([jnp.concatenate([keys[:, 0], zk], axis=-1),
                                 jnp.concatenate([zk, keys[:, 1]], axis=-1)], axis=1)
        s2, b, a, tau = _peer_route(x2d, pk_w_query[i].astype(BF16), keys_padded)
        x2d = _peer_dense(x2d, pk_u[i].astype(BF16), pk_v[i].astype(BF16).T, s2, b, a, tau,
                          row(ln2_g[i]), row(ln2_b[i]), alpha)
    return x2d.reshape(batch, seq, d_model).astype(x.dtype)
```

```python
import functools
import math

import jax
import jax.numpy as jnp
from jax import lax
from jax.experimental import pallas as pl
from jax.experimental.pallas import tpu as pltpu

F32 = jnp.float32
BF16 = jnp.bfloat16

LANES = 128
HEAD_DIM = 64
ROPE_THETA = 10000.0
LN_EPS = 1e-5
NEG = -1e30
BIG = 1e30

PK_HEADS = 8
PK_KEYS = 128
PK_TOPK = 16
SW_WINDOW = 128
VMEM_LIMIT = 48 * 1024 * 1024

_NT = (((1,), (1,)), ((), ()))


def _cparams(n_axes, flags=None):
    return pltpu.CompilerParams(dimension_semantics=("arbitrary",) * n_axes,
                                vmem_limit_bytes=VMEM_LIMIT, flags=flags)


def _qkv_kernel(x_ref, w_ref, b_ref, c_ref, s1_ref, s2_ref, o_ref, vt_ref, *,
                n_groups, n_q_groups, n_rope_groups, chunk, q_scale):
    xb = x_ref[...].astype(BF16)
    cos = c_ref[...]
    sin_lo = s1_ref[...]
    sin_hi = s2_ref[...]
    n_main = n_groups - vt_ref.shape[1]
    for c0 in range(0, n_groups, chunk):
        ng = min(chunk, n_groups - c0)
        cols = slice(c0 * LANES, (c0 + ng) * LANES)
        acc = jnp.dot(xb, w_ref[:, cols], preferred_element_type=F32) + b_ref[:, cols]
        for g in range(ng):
            gg = c0 + g
            a = acc[:, g * LANES:(g + 1) * LANES]
            if gg < n_rope_groups:
                a = a * cos + pltpu.roll(a, 96, 1) * sin_lo + pltpu.roll(a, 32, 1) * sin_hi
            if gg < n_q_groups:
                a = a * q_scale
            if gg < n_main:
                o_ref[:, gg * LANES:(gg + 1) * LANES] = a.astype(BF16)
            else:
                vt_ref[0, gg - n_main, 0] = a.T.astype(BF16)


def _qkv_rope(x2d, w, bias, tables, seq, n_q_groups, n_rope_groups, n_v_groups, q_scale, tm=512):
    T, D = x2d.shape
    N = w.shape[1]
    n_groups = N // LANES
    n_main = n_groups - n_v_groups
    pos_blocks = seq // tm
    kern = functools.partial(_qkv_kernel, n_groups=n_groups, n_q_groups=n_q_groups,
                             n_rope_groups=n_rope_groups, chunk=4, q_scale=q_scale)
    tab_spec = pl.BlockSpec((tm, LANES), lambda i: (i % pos_blocks, 0))
    return pl.pallas_call(
        kern,
        grid=(T // tm,),
        in_specs=[pl.BlockSpec((tm, D), lambda i: (i, 0)),
                  pl.BlockSpec((D, N), lambda i: (0, 0)),
                  pl.BlockSpec((1, N), lambda i: (0, 0)),
                  tab_spec, tab_spec, tab_spec],
        out_specs=[pl.BlockSpec((tm, n_main * LANES), lambda i: (i, 0)),
                   pl.BlockSpec((1, n_v_groups, 1, LANES, tm),
                                lambda i: (i // pos_blocks, 0, i % pos_blocks, 0, 0))],
        out_shape=[jax.ShapeDtypeStruct((T, n_main * LANES), BF16),
                   jax.ShapeDtypeStruct((T // seq, n_v_groups, pos_blocks, LANES, tm), BF16)],
        compiler_params=_cparams(1),
        name="qkv_rope",
    )(x2d, w, bias, *tables)


def _diff_attn_kernel(lam_ref, g_ref, q_ref, qn_ref, k_ref, vt_ref, o_ref, s_scr, *, tq, tk, lambda_init):
    qi = pl.program_id(2)
    lane = lax.broadcasted_iota(jnp.int32, (tq, LANES), 1)

    def stack_maps(q):
        zero = jnp.zeros_like(q)
        return jnp.concatenate([jnp.where(lane < HEAD_DIM, q, zero),
                                jnp.where(lane >= HEAD_DIM, q, zero)], axis=0)

    qs = stack_maps(q_ref[0])
    nq = 2 * tq

    def put_scores(slot, j, queries=None):
        kb = k_ref[0, pl.ds(pl.multiple_of(j * tk, tk), tk), :]
        s_scr[slot] = lax.dot_general(kb, qs if queries is None else queries, _NT,
                                      preferred_element_type=F32)

    def update(slot, j, carry, masked):
        m, l, acc = carry
        s = s_scr[slot]
        if masked:
            key = j * tk + lax.broadcasted_iota(jnp.int32, (tk, nq), 0)
            col = lax.broadcasted_iota(jnp.int32, (tk, nq), 1)
            qpos = qi * tq + jnp.where(col >= tq, col - tq, col)
            s = jnp.where(key <= qpos, s, NEG)
        m_new = jnp.maximum(m, jnp.max(s, axis=0, keepdims=True))
        a = jnp.exp2(m - m_new)
        p = jnp.exp2(s - m_new)
        l = a * l + jnp.sum(p, axis=0, keepdims=True)
        acc = a * acc + jnp.dot(vt_ref[0, 0, j], p.astype(BF16), preferred_element_type=F32)
        return m_new, l, acc

    def pair(i, carry):
        j = 2 * i
        put_scores(1, j + 1)
        carry = update(0, j, carry, False)
        put_scores(0, j + 2)
        return update(1, j + 1, carry, False)

    def tail_even(carry):
        return update(0, n_full, carry, True)

    def tail_odd(carry):
        put_scores(1, n_full)
        return update(1, n_full, update(0, n_full - 1, carry, False), True)

    init = (jnp.full((1, nq), NEG, F32), jnp.zeros((1, nq), F32), jnp.zeros((LANES, nq), F32))
    n_full = (qi * tq) // tk

    @pl.when(qi == 0)
    def _():
        put_scores(0, 0)

    carry = lax.fori_loop(0, n_full // 2, pair, init)
    m, l, acc = lax.cond(n_full % 2 == 1, tail_odd, tail_even, carry)
    put_scores(0, 0, stack_maps(qn_ref[0]))

    lp = lam_ref[...]
    lam = (jnp.exp(jnp.sum(lp[0:1] * lp[1:2], axis=1, keepdims=True))
           - jnp.exp(jnp.sum(lp[2:3] * lp[3:4], axis=1, keepdims=True)) + lambda_init)
    o = acc / l
    od = o[:, :tq] - lam * o[:, tq:]
    ms = jnp.mean(od * od, axis=0, keepdims=True)
    y = od * lax.rsqrt(ms + LN_EPS) * g_ref[...] * (1.0 - lambda_init)
    o_ref[0] = y.T.astype(BF16)


def _diff_attention(qk, vt, lam_params, subln_g, batch, seq, n_heads, lambda_init, tq=512):
    tk = vt.shape[-1]
    assert tq == tk
    T, N = qk.shape
    d_model = n_heads * LANES
    qkv3 = qk.reshape(batch, seq, N)
    kern = functools.partial(_diff_attn_kernel, tq=tq, tk=tk, lambda_init=lambda_init)
    out = pl.pallas_call(
        kern,
        grid=(batch, n_heads, seq // tq),
        in_specs=[pl.BlockSpec((4, HEAD_DIM), lambda b, h, i: (0, 0)),
                  pl.BlockSpec((LANES, 1), lambda b, h, i: (0, 0)),
                  pl.BlockSpec((1, tq, LANES), lambda b, h, i: (b, i, h)),
                  pl.BlockSpec((1, tq, LANES), lambda b, h, i: (b, jnp.minimum(i + 1, seq // tq - 1), h)),
                  pl.BlockSpec((1, seq, LANES), lambda b, h, i: (b, 0, n_heads + h)),
                  pl.BlockSpec((1, 1, seq // tk, LANES, tk), lambda b, h, i: (b, h, 0, 0, 0))],
        out_specs=pl.BlockSpec((1, tq, LANES), lambda b, h, i: (b, i, h)),
        out_shape=jax.ShapeDtypeStruct((batch, seq, d_model), BF16),
        scratch_shapes=[pltpu.VMEM((2, tk, 2 * tq), F32)],
        compiler_params=_cparams(3),
        name="diff_attention",
    )(lam_params.astype(F32), subln_g.astype(F32).reshape(LANES, 1), qkv3, qkv3, qkv3, vt)
    return out.reshape(T, d_model)


def _swa_kernel(sink_ref, q_ref, kc_ref, kp_ref, ksc_ref, ksp_ref,
                vtc_ref, vtp_ref, vtsc_ref, vtsp_ref, o_ref, *, blk, n_kv, group):
    n = pl.program_id(1)
    pairs = group // 2
    nq = pairs * blk
    lo_lane = lax.broadcasted_iota(jnp.int32, (blk, LANES), 1) < HEAD_DIM
    lo_feat = lax.broadcasted_iota(jnp.int32, (LANES, blk), 0) < HEAD_DIM
    k_nat = jnp.concatenate([kp_ref[0], kc_ref[0]], axis=0)
    k_swp = jnp.concatenate([ksp_ref[0], ksc_ref[0]], axis=0)
    vt_nat = jnp.concatenate([vtp_ref[0, 0, 0], vtc_ref[0, 0, 0]], axis=1)
    vt_swp = jnp.concatenate([vtsp_ref[0, 0, 0], vtsc_ref[0, 0, 0]], axis=1)
    key = lax.broadcasted_iota(jnp.int32, (2 * blk, nq), 0)
    col = lax.broadcasted_iota(jnp.int32, (2 * blk, nq), 1)
    rel = (col & (blk - 1)) + blk - key
    first_key = jnp.where(n > 0, 0, blk)
    valid = (rel >= 0) & (rel < SW_WINDOW) & (key >= first_key)
    for kv in range(n_kv):
        o_par = []
        for parity in range(2):
            use_nat = (kv == 0) == (parity == 0)
            kx = k_nat if use_nat else k_swp
            vtx = vt_nat if use_nat else vt_swp
            qm, sk = [], []
            for p in range(pairs):
                gcol = (kv * pairs + p) * LANES
                qg = q_ref[0, :, gcol:gcol + LANES]
                keep = lo_lane if parity == 0 else ~lo_lane
                qm.append(jnp.where(keep, qg, jnp.zeros_like(qg)))
                head = kv * group + 2 * p + parity
                sk.append(sink_ref[head:head + 1, :])
            qst = jnp.concatenate(qm, axis=0)
            sink = jnp.concatenate(sk, axis=1)
            s = lax.dot_general(kx, qst, _NT, preferred_element_type=F32)
            s = jnp.where(valid, s, NEG)
            m = jnp.maximum(jnp.max(s, axis=0, keepdims=True), sink)
            e = jnp.exp2(s - m)
            den = jnp.sum(e, axis=0, keepdims=True) + jnp.exp2(sink - m)
            prob = (e * (1.0 / den)).astype(BF16)
            o_par.append(jnp.dot(vtx, prob, preferred_element_type=F32))
        for p in range(pairs):
            gcol = (kv * pairs + p) * LANES
            cols = slice(p * blk, (p + 1) * blk)
            ot = jnp.where(lo_feat, o_par[0][:, cols], o_par[1][:, cols])
            o_ref[0, :, gcol:gcol + LANES] = ot.T.astype(BF16)


def _swa_attention(qkv, vt, sinks, batch, seq, n_heads, n_kv, blk=128):
    T, N = qkv.shape
    d_model = n_heads * HEAD_DIM
    qg = d_model // LANES
    qkv3 = qkv.reshape(batch, seq, N)
    nb = seq // blk
    per_tile = vt.shape[-1] // blk
    sink_b = jnp.broadcast_to((sinks.astype(F32) * math.log2(math.e))[:, None], (n_heads, LANES))

    def cur(col):
        return pl.BlockSpec((1, blk, LANES), lambda b, n: (b, n, col))

    def prev(col):
        return pl.BlockSpec((1, blk, LANES), lambda b, n: (b, jnp.maximum(n - 1, 0), col))

    def vt_spec(g, back):
        def index(b, n):
            m = jnp.maximum(n - back, 0)
            return (b, g, m // per_tile, 0, m % per_tile)
        return pl.BlockSpec((1, 1, 1, LANES, blk), index)

    kern = functools.partial(_swa_kernel, blk=blk, n_kv=n_kv, group=n_heads // n_kv)
    out = pl.pallas_call(
        kern,
        grid=(batch, nb),
        in_specs=[pl.BlockSpec((n_heads, LANES), lambda b, n: (0, 0)),
                  pl.BlockSpec((1, blk, d_model), lambda b, n: (b, n, 0)),
                  cur(qg), prev(qg), cur(qg + 1), prev(qg + 1),
                  vt_spec(0, 0), vt_spec(0, 1), vt_spec(1, 0), vt_spec(1, 1)],
        out_specs=pl.BlockSpec((1, blk, d_model), lambda b, n: (b, n, 0)),
        out_shape=jax.ShapeDtypeStruct((batch, seq, d_model), BF16),
        compiler_params=_cparams(2),
        name="swa_attention",
    )(sink_b, qkv3, qkv3, qkv3, qkv3, qkv3, vt, vt, vt, vt)
    return out.reshape(T, d_model)


def _layer_norm(y, g, b):
    mu = jnp.mean(y, axis=-1, keepdims=True)
    yc = y - mu
    var = jnp.mean(yc * yc, axis=-1, keepdims=True)
    return yc * lax.rsqrt(var + LN_EPS) * g + b


def _proj_ln_kernel(x_ref, o_ref, w_ref, b_ref, g_ref, beta_ref, out_ref, *, alpha):
    mix = jnp.dot(o_ref[...], w_ref[...], preferred_element_type=F32) + b_ref[...]
    out_ref[...] = _layer_norm(alpha * x_ref[...] + mix, g_ref[...], beta_ref[...])


def _proj_ln(x2d, o2d, w, bias, g, beta, alpha, tm=512):
    T, D = x2d.shape
    K = o2d.shape[1]
    row = lambda i: (i, 0)
    fixed = lambda i: (0, 0)
    return pl.pallas_call(
        functools.partial(_proj_ln_kernel, alpha=alpha),
        grid=(T // tm,),
        in_specs=[pl.BlockSpec((tm, D), row), pl.BlockSpec((tm, K), row),
                  pl.BlockSpec((K, D), fixed), pl.BlockSpec((1, D), fixed),
                  pl.BlockSpec((1, D), fixed), pl.BlockSpec((1, D), fixed)],
        out_specs=pl.BlockSpec((tm, D), row),
        out_shape=jax.ShapeDtypeStruct((T, D), F32),
        compiler_params=_cparams(1),
        name="proj_ln",
    )(x2d, o2d, w, bias, g, beta)


def _tree(fn, xs):
    xs = list(xs)
    while len(xs) > 1:
        nxt = [fn(xs[i], xs[i + 1]) for i in range(0, len(xs) - 1, 2)]
        if len(xs) % 2:
            nxt.append(xs[-1])
        xs = nxt
    return xs[0]


def _pair_words(v):
    bits = pltpu.bitcast(v.astype(BF16).astype(F32), jnp.uint32)
    return bits | (bits >> 16)


def _sort_network(n):
    pairs = []
    p = 1
    while p < n:
        k = p
        while k >= 1:
            for j in range(k % p, n - k, 2 * k):
                for i in range(min(k, n - j - k)):
                    if (i + j) // (2 * p) == (i + j + k) // (2 * p):
                        pairs.append((i + j, i + j + k))
            k //= 2
        p *= 2
    return tuple(pairs)


_SORT_TOPK = _sort_network(PK_TOPK)
SUBLANES = 8


def _exchange(x, i, j):
    x[i], x[j] = jnp.maximum(x[i], x[j]), jnp.minimum(x[i], x[j])


def _top16_sorted(x):
    x = list(x)
    for i, j in _SORT_TOPK:
        _exchange(x, i, j)
    for shift in (4, 2, 1):
        y = [pltpu.roll(v, shift, 0) for v in x]
        x = [jnp.maximum(x[i], y[PK_TOPK - 1 - i]) for i in range(PK_TOPK)]
        d = PK_TOPK // 2
        while d >= 1:
            for i in range(PK_TOPK):
                if not i & d:
                    _exchange(x, i, i + d)
            d //= 2
    return x


def _peer_route_kernel(x_ref, wq_ref, kp_ref, rk_o, b_o, n_o, a_o,
                       q_scr, sc_scr, top_scr, st_scr, sg_scr):
    tm = x_ref.shape[0]
    q = jnp.dot(x_ref[...].astype(BF16), wq_ref[...], preferred_element_type=F32)
    for h in range(PK_HEADS):
        q_scr[h] = q[:, h * LANES:(h + 1) * LANES].astype(BF16)

    def scores_and_tops(h, carry):
        qh = q_scr[h]
        for c in range(2):
            sc = lax.dot_general(kp_ref[h, c], qh, _NT, preferred_element_type=F32)
            sc_scr[h, c] = sc
            tops = [[] for _ in range(PK_TOPK)]
            for lg in range(tm // LANES):
                slabs = [sc[SUBLANES * k:SUBLANES * (k + 1), lg * LANES:(lg + 1) * LANES]
                         for k in range(PK_KEYS // SUBLANES)]
                for k, v in enumerate(_top16_sorted(slabs)):
                    tops[k].append(v[0:1, :])
            for k in range(PK_TOPK):
                top_scr[c, k, pl.ds(h, 1), :] = jnp.concatenate(tops[k], axis=1)
        return carry

    lax.fori_loop(0, PK_HEADS, scores_and_tops, 0)

    t1 = [top_scr[0, k] for k in range(PK_TOPK)]
    t2 = [top_scr[1, k] for k in range(PK_TOPK)]
    cand = {(r, j): t1[r] + t2[j] for r in range(PK_TOPK) for j in range(PK_TOPK)
            if (r + 1) * (j + 1) <= PK_TOPK}
    cur = list(cand.values())
    thr = None
    for k in range(PK_TOPK):
        thr = _tree(jnp.maximum, cur)
        if k + 1 < PK_TOPK:
            cur = [jnp.where(c == thr, NEG, c) for c in cur]
    cmax = cand[(0, 0)]
    z = _tree(jnp.add, [jnp.where(c >= thr, jnp.exp(c - cmax), 0.0) for c in cand.values()])
    st_scr[0] = 0.5 / z
    for j in range(PK_TOPK):
        sg = jnp.full(thr.shape, BIG, F32)
        for r in range(PK_TOPK):
            if (r, j) in cand:
                sg = jnp.where(cand[(r, j)] >= thr, t1[r], sg)
        sg_scr[j] = sg

    def per_head(h, carry):
        s1 = sc_scr[h, 0]
        s2 = sc_scr[h, 1]
        rank = jnp.zeros(s2.shape, F32)
        n = jnp.zeros(s1.shape, F32)
        for k in range(PK_TOPK):
            rank = jnp.where(top_scr[1, k, pl.ds(h, 1), :] > s2, k + 1.0, rank)
            n = jnp.where(s1 >= sg_scr[k, pl.ds(h, 1), :], k + 1.0, n)
        rk_o[h] = pltpu.bitcast(rank.astype(BF16), jnp.uint32)
        n_o[h] = _pair_words(n)
        a_o[h] = _pair_words(jnp.exp(s1 - top_scr[0, 0, pl.ds(h, 1), :]))
        b = jnp.exp(s2 - top_scr[1, 0, pl.ds(h, 1), :]) * st_scr[0, pl.ds(h, 1), :]
        b_o[h] = pltpu.bitcast(b.astype(BF16), jnp.uint32)
        return carry

    lax.fori_loop(0, PK_HEADS, per_head, 0)


def _peer_route(x2d, wq, keys_padded, tm=256):
    T, D = x2d.shape
    half = jax.ShapeDtypeStruct((PK_HEADS, PK_KEYS // 2, T), jnp.uint32)
    word = jax.ShapeDtypeStruct((PK_HEADS, PK_KEYS, T), jnp.uint32)
    hspec = pl.BlockSpec((PK_HEADS, PK_KEYS // 2, tm), lambda i: (0, 0, i))
    ospec = pl.BlockSpec((PK_HEADS, PK_KEYS, tm), lambda i: (0, 0, i))
    return pl.pallas_call(
        _peer_route_kernel,
        grid=(T // tm,),
        in_specs=[pl.BlockSpec((tm, D), lambda i: (i, 0)),
                  pl.BlockSpec(wq.shape, lambda i: (0, 0)),
                  pl.BlockSpec(keys_padded.shape, lambda i: (0, 0, 0, 0))],
        out_specs=[hspec, hspec, ospec, ospec],
        out_shape=[half, half, word, word],
        scratch_shapes=[pltpu.VMEM((PK_HEADS, tm, LANES), BF16),
                        pltpu.VMEM((PK_HEADS, 2, PK_KEYS, tm), F32),
                        pltpu.VMEM((2, PK_TOPK, PK_HEADS, tm), F32),
                        pltpu.VMEM((1, PK_HEADS, tm), F32),
                        pltpu.VMEM((PK_TOPK, PK_HEADS, tm), F32)],
        compiler_params=_cparams(1),
        name="peer_route",
    )(x2d, wq, keys_padded)


BF16_ROWS = 16


def _peer_dense_kernel(x_ref, u_ref, vt_ref, rk_ref, b_ref, n_ref, a_ref, g_ref, beta_ref,
                       out_ref, xb_scr, w_scr, acc_scr, *, alpha, sub):
    j = pl.program_id(1)
    nj = pl.num_programs(1)
    ec, tt = 2 * w_scr.shape[0], w_scr.shape[1]

    @pl.when(j == 0)
    def _():
        xb_scr[...] = x_ref[...].T.astype(BF16)
        acc_scr[...] = jnp.zeros_like(acc_scr)

    xt = xb_scr[...]
    for k in range(ec // sub):
        ek = slice(k * sub, (k + 1) * sub)
        hk = jnp.dot(u_ref[ek, :], xt, preferred_element_type=F32)
        keys1 = range(k * (sub // PK_KEYS), (k + 1) * (sub // PK_KEYS))
        for lg in range(tt // LANES):
            lanes = slice(lg * LANES, (lg + 1) * LANES)
            splat = lambda ref, h, i1: pltpu.bitcast(
                jnp.broadcast_to(ref[h, i1:i1 + 1, lanes], (8, LANES)), BF16)
            n_tiles = PK_KEYS // BF16_ROWS
            gates = [[None] * n_tiles for _ in keys1]
            for h in range(PK_HEADS):
                nb = [splat(n_ref, h, i1) for i1 in keys1]
                ab = [splat(a_ref, h, i1) for i1 in keys1]
                for r in range(n_tiles):
                    words = slice(r * 8, (r + 1) * 8)
                    bt = pltpu.bitcast(b_ref[h, words, lanes], BF16)
                    rk = pltpu.bitcast(rk_ref[h, words, lanes], BF16)
                    zero = jnp.zeros_like(bt)
                    for i1l in range(len(keys1)):
                        term = jnp.where(rk < nb[i1l], bt, zero) * ab[i1l]
                        gates[i1l][r] = term if h == 0 else gates[i1l][r] + term
            for i1l in range(len(keys1)):
                for r in range(n_tiles):
                    row0 = i1l * PK_KEYS + r * BF16_ROWS
                    hh = hk[row0:row0 + BF16_ROWS, lanes]
                    act = hh * (1.0 + lax.erf(hh * (2.0 ** -0.5)))
                    w0 = (k * sub + row0) // 2
                    w_scr[w0:w0 + 8, lanes] = pltpu.bitcast(gates[i1l][r] * act.astype(BF16), jnp.uint32)
    w_all = pltpu.bitcast(w_scr[...], BF16)
    acc_scr[...] += jnp.dot(vt_ref[...], w_all, preferred_element_type=F32)

    @pl.when(j == nj - 1)
    def _():
        y = alpha * x_ref[...] + acc_scr[...].T
        out_ref[...] = _layer_norm(y, g_ref[...], beta_ref[...])


def _peer_dense(x2d, u, vt, rk, b, n, a, g, beta, alpha, tt=512, ec=2048, sub=256):
    T, D = x2d.shape
    n_exp = u.shape[0]
    keys_per_chunk = ec // PK_KEYS
    tok = pl.BlockSpec((PK_HEADS, PK_KEYS // 2, tt), lambda i, j: (0, 0, i))
    key = pl.BlockSpec((PK_HEADS, keys_per_chunk, tt), lambda i, j: (0, j, i))
    fixed = pl.BlockSpec((1, D), lambda i, j: (0, 0))
    return pl.pallas_call(
        functools.partial(_peer_dense_kernel, alpha=alpha, sub=sub),
        grid=(T // tt, n_exp // ec),
        in_specs=[pl.BlockSpec((tt, D), lambda i, j: (i, 0)),
                  pl.BlockSpec((ec, D), lambda i, j: (j, 0)),
                  pl.BlockSpec((D, ec), lambda i, j: (0, j)),
                  tok, tok, key, key, fixed, fixed],
        out_specs=pl.BlockSpec((tt, D), lambda i, j: (i, 0)),
        out_shape=jax.ShapeDtypeStruct((T, D), F32),
        scratch_shapes=[pltpu.VMEM((D, tt), BF16),
                        pltpu.VMEM((ec // 2, tt), jnp.uint32),
                        pltpu.VMEM((D, tt), F32)],
        compiler_params=_cparams(2),
        name="peer_dense",
    )(x2d, u, vt, rk, b, n, a, g, beta)


def _rope_tables(seq):
    half = HEAD_DIM // 2
    inv = 1.0 / (ROPE_THETA ** (jnp.arange(0, HEAD_DIM, 2, dtype=F32) / HEAD_DIM))
    ang = jnp.arange(seq, dtype=F32)[:, None] * inv[None, :]
    cos = jnp.tile(jnp.cos(ang), (1, LANES // half))
    sin = jnp.tile(jnp.sin(ang), (1, LANES // half))
    first_half = (jnp.arange(LANES) % HEAD_DIM) < half
    return (cos, jnp.where(first_half[None, :], -sin, 0.0), jnp.where(first_half[None, :], 0.0, sin))


def _swap_halves(w):
    n = w.shape[-1] // 2
    return jnp.concatenate([w[..., n:], w[..., :n]], axis=-1)


def kernel(x, da_w_qkv, da_lambda, da_subln_g, da_w_o, sw_w_qkv, sw_b_qkv, sw_sinks, sw_w_o, sw_b_o,
           pk_w_query, pk_sub_keys, pk_u, pk_v, ln1_g, ln1_b, ln2_g, ln2_b):
    batch, seq, d_model = x.shape
    depth = pk_w_query.shape[0]
    T = batch * seq
    alpha = (2 * depth) ** 0.25
    tables = _rope_tables(seq)
    da_heads = d_model // (2 * HEAD_DIM)
    sw_heads = d_model // HEAD_DIM
    sw_kv = (sw_w_qkv.shape[-1] - d_model) // (2 * HEAD_DIM)
    assert sw_kv == 2 and sw_kv * HEAD_DIM == LANES
    row = lambda v: v.astype(F32).reshape(1, -1)

    x2d = x.reshape(T, d_model).astype(F32)
    for i in range(depth):
        j = i // 2
        if i % 2 == 0:
            lambda_init = 0.8 - 0.6 * math.exp(-0.3 * i)
            w = da_w_qkv[j].astype(BF16)
            qk, vt = _qkv_rope(x2d, w, jnp.zeros((1, w.shape[1]), F32), tables, seq,
                               n_q_groups=da_heads, n_rope_groups=2 * da_heads, n_v_groups=da_heads,
                               q_scale=HEAD_DIM ** -0.5 * math.log2(math.e))
            mix_in = _diff_attention(qk, vt, da_lambda[j], da_subln_g[j], batch, seq, da_heads, lambda_init)
            w_o, b_o = da_w_o[j].astype(BF16), jnp.zeros((1, d_model), F32)
        else:
            wq_, wk_, wv_ = (sw_w_qkv[j][:, :d_model], sw_w_qkv[j][:, d_model:d_model + LANES],
                             sw_w_qkv[j][:, d_model + LANES:])
            bq_, bk_, bv_ = (sw_b_qkv[j][:d_model], sw_b_qkv[j][d_model:d_model + LANES],
                             sw_b_qkv[j][d_model + LANES:])
            w = jnp.concatenate([wq_, wk_, _swap_halves(wk_), wv_, _swap_halves(wv_)], axis=1).astype(BF16)
            bias = jnp.concatenate([bq_, bk_, _swap_halves(bk_), bv_, _swap_halves(bv_)]).astype(F32)
            n_qg = d_model // LANES
            qk, vt = _qkv_rope(x2d, w, bias.reshape(1, -1), tables, seq,
                               n_q_groups=n_qg, n_rope_groups=n_qg + 2, n_v_groups=2,
                               q_scale=HEAD_DIM ** -0.5 * math.log2(math.e))
            mix_in = _swa_attention(qk, vt, sw_sinks[j], batch, seq, sw_heads, sw_kv)
            w_o, b_o = sw_w_o[j].astype(BF16), row(sw_b_o[j])
        x2d = _proj_ln(x2d, mix_in, w_o, b_o, row(ln1_g[i]), row(ln1_b[i]), alpha)

        keys = pk_sub_keys[i].astype(BF16)
        zk = jnp.zeros_like(keys[:, 0])
        keys_padded = jnp.stack([jnp.concatenate([keys[:, 0], zk], axis=-1),
                                 jnp.concatenate([zk, keys[:, 1]], axis=-1)], axis=1)
        rk, b, n, a = _peer_route(x2d, pk_w_query[i].astype(BF16), keys_padded)
        x2d = _peer_dense(x2d, pk_u[i].astype(BF16), pk_v[i].astype(BF16).T, rk, b, n, a,
                          row(ln2_g[i]), row(ln2_b[i]), alpha)
    return x2d.reshape(batch, seq, d_model).astype(x.dtype)
```

```python
import functools
import math

import jax
import jax.numpy as jnp
from jax import lax
from jax.experimental import pallas as pl
from jax.experimental.pallas import tpu as pltpu

F32 = jnp.float32
BF16 = jnp.bfloat16

LANES = 128
HEAD_DIM = 64
ROPE_THETA = 10000.0
LN_EPS = 1e-5
NEG = -1e30
BIG = 1e30

PK_HEADS = 8
PK_KEYS = 128
PK_TOPK = 16
SW_WINDOW = 128
VMEM_LIMIT = 48 * 1024 * 1024

_NT = (((1,), (1,)), ((), ()))


def _cparams(n_axes, flags=None):
    return pltpu.CompilerParams(dimension_semantics=("arbitrary",) * n_axes,
                                vmem_limit_bytes=VMEM_LIMIT, flags=flags)


def _qkv_kernel(x_ref, w_ref, b_ref, c_ref, s1_ref, s2_ref, o_ref, vt_ref, *,
                n_groups, n_q_groups, n_rope_groups, chunk, q_scale):
    xb = x_ref[...].astype(BF16)
    cos = c_ref[...]
    sin_lo = s1_ref[...]
    sin_hi = s2_ref[...]
    n_main = n_groups - vt_ref.shape[1]
    for c0 in range(0, n_groups, chunk):
        ng = min(chunk, n_groups - c0)
        cols = slice(c0 * LANES, (c0 + ng) * LANES)
        acc = jnp.dot(xb, w_ref[:, cols], preferred_element_type=F32) + b_ref[:, cols]
        for g in range(ng):
            gg = c0 + g
            a = acc[:, g * LANES:(g + 1) * LANES]
            if gg < n_rope_groups:
                a = a * cos + pltpu.roll(a, 96, 1) * sin_lo + pltpu.roll(a, 32, 1) * sin_hi
            if gg < n_q_groups:
                a = a * q_scale
            if gg < n_main:
                o_ref[:, gg * LANES:(gg + 1) * LANES] = a.astype(BF16)
            else:
                vt_ref[0, gg - n_main, 0] = a.T.astype(BF16)


def _qkv_rope(x2d, w, bias, tables, seq, n_q_groups, n_rope_groups, n_v_groups, q_scale, tm=512):
    T, D = x2d.shape
    N = w.shape[1]
    n_groups = N // LANES
    n_main = n_groups - n_v_groups
    pos_blocks = seq // tm
    kern = functools.partial(_qkv_kernel, n_groups=n_groups, n_q_groups=n_q_groups,
                             n_rope_groups=n_rope_groups, chunk=4, q_scale=q_scale)
    tab_spec = pl.BlockSpec((tm, LANES), lambda i: (i % pos_blocks, 0))
    return pl.pallas_call(
        kern,
        grid=(T // tm,),
        in_specs=[pl.BlockSpec((tm, D), lambda i: (i, 0)),
                  pl.BlockSpec((D, N), lambda i: (0, 0)),
                  pl.BlockSpec((1, N), lambda i: (0, 0)),
                  tab_spec, tab_spec, tab_spec],
        out_specs=[pl.BlockSpec((tm, n_main * LANES), lambda i: (i, 0)),
                   pl.BlockSpec((1, n_v_groups, 1, LANES, tm),
                                lambda i: (i // pos_blocks, 0, i % pos_blocks, 0, 0))],
        out_shape=[jax.ShapeDtypeStruct((T, n_main * LANES), BF16),
                   jax.ShapeDtypeStruct((T // seq, n_v_groups, pos_blocks, LANES, tm), BF16)],
        compiler_params=_cparams(1),
        name="qkv_rope",
    )(x2d, w, bias, *tables)


def _diff_attn_kernel(lam_ref, g_ref, q_ref, qn_ref, k_ref, vt_ref, o_ref, s_scr, *, tq, tk, lambda_init):
    qi = pl.program_id(2)
    lane = lax.broadcasted_iota(jnp.int32, (tq, LANES), 1)

    def stack_maps(q):
        zero = jnp.zeros_like(q)
        return jnp.concatenate([jnp.where(lane < HEAD_DIM, q, zero),
                                jnp.where(lane >= HEAD_DIM, q, zero)], axis=0)

    qs = stack_maps(q_ref[0])
    nq = 2 * tq

    def put_scores(slot, j, queries=None):
        kb = k_ref[0, pl.ds(pl.multiple_of(j * tk, tk), tk), :]
        s_scr[slot] = lax.dot_general(kb, qs if queries is None else queries, _NT,
                                      preferred_element_type=F32)

    def update(slot, j, carry, masked):
        m, l, acc = carry
        s = s_scr[slot]
        if masked:
            key = j * tk + lax.broadcasted_iota(jnp.int32, (tk, nq), 0)
            col = lax.broadcasted_iota(jnp.int32, (tk, nq), 1)
            qpos = qi * tq + jnp.where(col >= tq, col - tq, col)
            s = jnp.where(key <= qpos, s, NEG)
        m_new = jnp.maximum(m, jnp.max(s, axis=0, keepdims=True))
        a = jnp.exp2(m - m_new)
        p = jnp.exp2(s - m_new)
        l = a * l + jnp.sum(p, axis=0, keepdims=True)
        acc = a * acc + jnp.dot(vt_ref[0, 0, j], p.astype(BF16), preferred_element_type=F32)
        return m_new, l, acc

    def pair(i, carry):
        j = 2 * i
        put_scores(1, j + 1)
        carry = update(0, j, carry, False)
        put_scores(0, j + 2)
        return update(1, j + 1, carry, False)

    def tail_even(carry):
        return update(0, n_full, carry, True)

    def tail_odd(carry):
        put_scores(1, n_full)
        return update(1, n_full, update(0, n_full - 1, carry, False), True)

    init = (jnp.full((1, nq), NEG, F32), jnp.zeros((1, nq), F32), jnp.zeros((LANES, nq), F32))
    n_full = (qi * tq) // tk

    @pl.when(qi == 0)
    def _():
        put_scores(0, 0)

    carry = lax.fori_loop(0, n_full // 2, pair, init)
    m, l, acc = lax.cond(n_full % 2 == 1, tail_odd, tail_even, carry)
    put_scores(0, 0, stack_maps(qn_ref[0]))

    lp = lam_ref[...]
    lam = (jnp.exp(jnp.sum(lp[0:1] * lp[1:2], axis=1, keepdims=True))
           - jnp.exp(jnp.sum(lp[2:3] * lp[3:4], axis=1, keepdims=True)) + lambda_init)
    o = acc / l
    od = o[:, :tq] - lam * o[:, tq:]
    ms = jnp.mean(od * od, axis=0, keepdims=True)
    y = od * lax.rsqrt(ms + LN_EPS) * g_ref[...] * (1.0 - lambda_init)
    o_ref[0] = y.T.astype(BF16)


def _diff_attention(qk, vt, lam_params, subln_g, batch, seq, n_heads, lambda_init, tq=512):
    tk = vt.shape[-1]
    assert tq == tk
    T, N = qk.shape
    d_model = n_heads * LANES
    qkv3 = qk.reshape(batch, seq, N)
    kern = functools.partial(_diff_attn_kernel, tq=tq, tk=tk, lambda_init=lambda_init)
    out = pl.pallas_call(
        kern,
        grid=(batch, n_heads, seq // tq),
        in_specs=[pl.BlockSpec((4, HEAD_DIM), lambda b, h, i: (0, 0)),
                  pl.BlockSpec((LANES, 1), lambda b, h, i: (0, 0)),
                  pl.BlockSpec((1, tq, LANES), lambda b, h, i: (b, i, h)),
                  pl.BlockSpec((1, tq, LANES), lambda b, h, i: (b, jnp.minimum(i + 1, seq // tq - 1), h)),
                  pl.BlockSpec((1, seq, LANES), lambda b, h, i: (b, 0, n_heads + h)),
                  pl.BlockSpec((1, 1, seq // tk, LANES, tk), lambda b, h, i: (b, h, 0, 0, 0))],
        out_specs=pl.BlockSpec((1, tq, LANES), lambda b, h, i: (b, i, h)),
        out_shape=jax.ShapeDtypeStruct((batch, seq, d_model), BF16),
        scratch_shapes=[pltpu.VMEM((2, tk, 2 * tq), F32)],
        compiler_params=_cparams(3),
        name="diff_attention",
    )(lam_params.astype(F32), subln_g.astype(F32).reshape(LANES, 1), qkv3, qkv3, qkv3, vt)
    return out.reshape(T, d_model)


def _swa_kernel(sink_ref, q_ref, kc_ref, kp_ref, ksc_ref, ksp_ref,
                vtc_ref, vtp_ref, vtsc_ref, vtsp_ref, o_ref, *, blk, n_kv, group):
    n = pl.program_id(1)
    pairs = group // 2
    nq = pairs * blk
    lo_lane = lax.broadcasted_iota(jnp.int32, (blk, LANES), 1) < HEAD_DIM
    lo_feat = lax.broadcasted_iota(jnp.int32, (LANES, blk), 0) < HEAD_DIM
    k_nat = jnp.concatenate([kp_ref[0], kc_ref[0]], axis=0)
    k_swp = jnp.concatenate([ksp_ref[0], ksc_ref[0]], axis=0)
    vt_nat = jnp.concatenate([vtp_ref[0, 0, 0], vtc_ref[0, 0, 0]], axis=1)
    vt_swp = jnp.concatenate([vtsp_ref[0, 0, 0], vtsc_ref[0, 0, 0]], axis=1)
    key = lax.broadcasted_iota(jnp.int32, (2 * blk, nq), 0)
    col = lax.broadcasted_iota(jnp.int32, (2 * blk, nq), 1)
    rel = (col & (blk - 1)) + blk - key
    first_key = jnp.where(n > 0, 0, blk)
    valid = (rel >= 0) & (rel < SW_WINDOW) & (key >= first_key)
    for kv in range(n_kv):
        o_par = []
        for parity in range(2):
            use_nat = (kv == 0) == (parity == 0)
            kx = k_nat if use_nat else k_swp
            vtx = vt_nat if use_nat else vt_swp
            qm, sk = [], []
            for p in range(pairs):
                gcol = (kv * pairs + p) * LANES
                qg = q_ref[0, :, gcol:gcol + LANES]
                keep = lo_lane if parity == 0 else ~lo_lane
                qm.append(jnp.where(keep, qg, jnp.zeros_like(qg)))
                head = kv * group + 2 * p + parity
                sk.append(sink_ref[head:head + 1, :])
            qst = jnp.concatenate(qm, axis=0)
            sink = jnp.concatenate(sk, axis=1)
            s = lax.dot_general(kx, qst, _NT, preferred_element_type=F32)
            s = jnp.where(valid, s, NEG)
            m = jnp.maximum(jnp.max(s, axis=0, keepdims=True), sink)
            e = jnp.exp2(s - m)
            den = jnp.sum(e, axis=0, keepdims=True) + jnp.exp2(sink - m)
            prob = (e * (1.0 / den)).astype(BF16)
            o_par.append(jnp.dot(vtx, prob, preferred_element_type=F32))
        for p in range(pairs):
            gcol = (kv * pairs + p) * LANES
            cols = slice(p * blk, (p + 1) * blk)
            ot = jnp.where(lo_feat, o_par[0][:, cols], o_par[1][:, cols])
            o_ref[0, :, gcol:gcol + LANES] = ot.T.astype(BF16)


def _swa_attention(qkv, vt, sinks, batch, seq, n_heads, n_kv, blk=128):
    T, N = qkv.shape
    d_model = n_heads * HEAD_DIM
    qg = d_model // LANES
    qkv3 = qkv.reshape(batch, seq, N)
    nb = seq // blk
    per_tile = vt.shape[-1] // blk
    sink_b = jnp.broadcast_to((sinks.astype(F32) * math.log2(math.e))[:, None], (n_heads, LANES))

    def cur(col):
        return pl.BlockSpec((1, blk, LANES), lambda b, n: (b, n, col))

    def prev(col):
        return pl.BlockSpec((1, blk, LANES), lambda b, n: (b, jnp.maximum(n - 1, 0), col))

    def vt_spec(g, back):
        def index(b, n):
            m = jnp.maximum(n - back, 0)
            return (b, g, m // per_tile, 0, m % per_tile)
        return pl.BlockSpec((1, 1, 1, LANES, blk), index)

    kern = functools.partial(_swa_kernel, blk=blk, n_kv=n_kv, group=n_heads // n_kv)
    out = pl.pallas_call(
        kern,
        grid=(batch, nb),
        in_specs=[pl.BlockSpec((n_heads, LANES), lambda b, n: (0, 0)),
                  pl.BlockSpec((1, blk, d_model), lambda b, n: (b, n, 0)),
                  cur(qg), prev(qg), cur(qg + 1), prev(qg + 1),
                  vt_spec(0, 0), vt_spec(0, 1), vt_spec(1, 0), vt_spec(1, 1)],
        out_specs=pl.BlockSpec((1, blk, d_model), lambda b, n: (b, n, 0)),
        out_shape=jax.ShapeDtypeStruct((batch, seq, d_model), BF16),
        compiler_params=_cparams(2),
        name="swa_attention",
    )(sink_b, qkv3, qkv3, qkv3, qkv3, qkv3, vt, vt, vt, vt)
    return out.reshape(T, d_model)


def _layer_norm(y, g, b):
    mu = jnp.mean(y, axis=-1, keepdims=True)
    yc = y - mu
    var = jnp.mean(yc * yc, axis=-1, keepdims=True)
    return yc * lax.rsqrt(var + LN_EPS) * g + b


def _proj_ln_kernel(x_ref, o_ref, w_ref, b_ref, g_ref, beta_ref, out_ref, *, alpha):
    mix = jnp.dot(o_ref[...], w_ref[...], preferred_element_type=F32) + b_ref[...]
    out_ref[...] = _layer_norm(alpha * x_ref[...] + mix, g_ref[...], beta_ref[...])


def _proj_ln(x2d, o2d, w, bias, g, beta, alpha, tm=512):
    T, D = x2d.shape
    K = o2d.shape[1]
    row = lambda i: (i, 0)
    fixed = lambda i: (0, 0)
    return pl.pallas_call(
        functools.partial(_proj_ln_kernel, alpha=alpha),
        grid=(T // tm,),
        in_specs=[pl.BlockSpec((tm, D), row), pl.BlockSpec((tm, K), row),
                  pl.BlockSpec((K, D), fixed), pl.BlockSpec((1, D), fixed),
                  pl.BlockSpec((1, D), fixed), pl.BlockSpec((1, D), fixed)],
        out_specs=pl.BlockSpec((tm, D), row),
        out_shape=jax.ShapeDtypeStruct((T, D), F32),
        compiler_params=_cparams(1),
        name="proj_ln",
    )(x2d, o2d, w, bias, g, beta)


def _tree(fn, xs):
    xs = list(xs)
    while len(xs) > 1:
        nxt = [fn(xs[i], xs[i + 1]) for i in range(0, len(xs) - 1, 2)]
        if len(xs) % 2:
            nxt.append(xs[-1])
        xs = nxt
    return xs[0]


def _pair_words(v):
    bits = pltpu.bitcast(v.astype(BF16).astype(F32), jnp.uint32)
    return bits | (bits >> 16)


def _sort_network(n):
    pairs = []
    p = 1
    while p < n:
        k = p
        while k >= 1:
            for j in range(k % p, n - k, 2 * k):
                for i in range(min(k, n - j - k)):
                    if (i + j) // (2 * p) == (i + j + k) // (2 * p):
                        pairs.append((i + j, i + j + k))
            k //= 2
        p *= 2
    return tuple(pairs)


_SORT_TOPK = _sort_network(PK_TOPK)
SUBLANES = 8


def _exchange(x, i, j):
    x[i], x[j] = jnp.maximum(x[i], x[j]), jnp.minimum(x[i], x[j])


def _top16_sorted(x):
    x = list(x)
    for i, j in _SORT_TOPK:
        _exchange(x, i, j)
    for shift in (4, 2, 1):
        y = [pltpu.roll(v, shift, 0) for v in x]
        x = [jnp.maximum(x[i], y[PK_TOPK - 1 - i]) for i in range(PK_TOPK)]
        d = PK_TOPK // 2
        while d >= 1:
            for i in range(PK_TOPK):
                if not i & d:
                    _exchange(x, i, i + d)
            d //= 2
    return x


def _peer_route_kernel(x_ref, wq_ref, kp_ref, rk_o, b_o, n_o, a_o,
                       q_scr, sc_scr, top_scr, st_scr, sg_scr):
    tm = x_ref.shape[0]
    q = jnp.dot(x_ref[...].astype(BF16), wq_ref[...], preferred_element_type=F32)
    for h in range(PK_HEADS):
        q_scr[h] = q[:, h * LANES:(h + 1) * LANES].astype(BF16)

    def scores_and_tops(h, carry):
        qh = q_scr[h]
        for c in range(2):
            sc = lax.dot_general(kp_ref[h, c], qh, _NT, preferred_element_type=F32)
            sc_scr[h, c] = sc
            tops = [[] for _ in range(PK_TOPK)]
            for lg in range(tm // LANES):
                slabs = [sc[SUBLANES * k:SUBLANES * (k + 1), lg * LANES:(lg + 1) * LANES]
                         for k in range(PK_KEYS // SUBLANES)]
                for k, v in enumerate(_top16_sorted(slabs)):
                    tops[k].append(v[0:1, :])
            for k in range(PK_TOPK):
                top_scr[c, k, pl.ds(h, 1), :] = jnp.concatenate(tops[k], axis=1)
        return carry

    lax.fori_loop(0, PK_HEADS, scores_and_tops, 0)

    t1 = [top_scr[0, k] for k in range(PK_TOPK)]
    t2 = [top_scr[1, k] for k in range(PK_TOPK)]
    cand = {(r, j): t1[r] + t2[j] for r in range(PK_TOPK) for j in range(PK_TOPK)
            if (r + 1) * (j + 1) <= PK_TOPK}
    cur = list(cand.values())
    thr = None
    for k in range(PK_TOPK):
        thr = _tree(jnp.maximum, cur)
        if k + 1 < PK_TOPK:
            cur = [jnp.where(c == thr, NEG, c) for c in cur]
    cmax = cand[(0, 0)]
    z = _tree(jnp.add, [jnp.where(c >= thr, jnp.exp(c - cmax), 0.0) for c in cand.values()])
    st_scr[0] = 0.5 / z
    for j in range(PK_TOPK):
        sg = jnp.full(thr.shape, BIG, F32)
        for r in range(PK_TOPK):
            if (r, j) in cand:
                sg = jnp.where(cand[(r, j)] >= thr, t1[r], sg)
        sg_scr[j] = sg

    def per_head(h, carry):
        s1 = sc_scr[h, 0]
        s2 = sc_scr[h, 1]
        rank = jnp.zeros(s2.shape, F32)
        n = jnp.zeros(s1.shape, F32)
        for k in range(PK_TOPK):
            rank = jnp.where(top_scr[1, k, pl.ds(h, 1), :] > s2, k + 1.0, rank)
            n = jnp.where(s1 >= sg_scr[k, pl.ds(h, 1), :], k + 1.0, n)
        rk_o[h] = pltpu.bitcast(rank.astype(BF16), jnp.uint32)
        n_o[h] = _pair_words(n)
        a_o[h] = _pair_words(jnp.exp(s1 - top_scr[0, 0, pl.ds(h, 1), :]))
        b = jnp.exp(s2 - top_scr[1, 0, pl.ds(h, 1), :]) * st_scr[0, pl.ds(h, 1), :]
        b_o[h] = pltpu.bitcast(b.astype(BF16), jnp.uint32)
        return carry

    lax.fori_loop(0, PK_HEADS, per_head, 0)


def _peer_route(x2d, wq, keys_padded, tm=256):
    T, D = x2d.shape
    half = jax.ShapeDtypeStruct((PK_HEADS, PK_KEYS // 2, T), jnp.uint32)
    word = jax.ShapeDtypeStruct((PK_HEADS, PK_KEYS, T), jnp.uint32)
    hspec = pl.BlockSpec((PK_HEADS, PK_KEYS // 2, tm), lambda i: (0, 0, i))
    ospec = pl.BlockSpec((PK_HEADS, PK_KEYS, tm), lambda i: (0, 0, i))
    return pl.pallas_call(
        _peer_route_kernel,
        grid=(T // tm,),
        in_specs=[pl.BlockSpec((tm, D), lambda i: (i, 0)),
                  pl.BlockSpec(wq.shape, lambda i: (0, 0)),
                  pl.BlockSpec(keys_padded.shape, lambda i: (0, 0, 0, 0))],
        out_specs=[hspec, hspec, ospec, ospec],
        out_shape=[half, half, word, word],
        scratch_shapes=[pltpu.VMEM((PK_HEADS, tm, LANES), BF16),
                        pltpu.VMEM((PK_HEADS, 2, PK_KEYS, tm), F32),
                        pltpu.VMEM((2, PK_TOPK, PK_HEADS, tm), F32),
                        pltpu.VMEM((1, PK_HEADS, tm), F32),
                        pltpu.VMEM((PK_TOPK, PK_HEADS, tm), F32)],
        compiler_params=_cparams(1),
        name="peer_route",
    )(x2d, wq, keys_padded)


BF16_ROWS = 16


def _peer_dense_kernel(x_ref, u_ref, vt_ref, rk_ref, b_ref, n_ref, a_ref, g_ref, beta_ref,
                       out_ref, xb_scr, w_scr, acc_scr, *, alpha, sub):
    j = pl.program_id(1)
    nj = pl.num_programs(1)
    ec, tt = 2 * w_scr.shape[0], w_scr.shape[1]

    @pl.when(j == 0)
    def _():
        xb_scr[...] = x_ref[...].T.astype(BF16)
        acc_scr[...] = jnp.zeros_like(acc_scr)

    xt = xb_scr[...]
    for k in range(ec // sub):
        uk = pltpu.bitcast(u_ref[k * sub // 2:(k + 1) * sub // 2, :], BF16)
        hk = jnp.dot(uk, xt, preferred_element_type=F32)
        keys1 = range(k * (sub // PK_KEYS), (k + 1) * (sub // PK_KEYS))
        for lg in range(tt // LANES):
            lanes = slice(lg * LANES, (lg + 1) * LANES)
            splat = lambda ref, h, i1: pltpu.bitcast(
                jnp.broadcast_to(ref[h, i1:i1 + 1, lanes], (8, LANES)), BF16)
            n_tiles = PK_KEYS // BF16_ROWS
            gates = [[None] * n_tiles for _ in keys1]
            for h in range(PK_HEADS):
                nb = [splat(n_ref, h, i1) for i1 in keys1]
                ab = [splat(a_ref, h, i1) for i1 in keys1]
                for r in range(n_tiles):
                    words = slice(r * 8, (r + 1) * 8)
                    bt = pltpu.bitcast(b_ref[h, words, lanes], BF16)
                    rk = pltpu.bitcast(rk_ref[h, words, lanes], BF16)
                    zero = jnp.zeros_like(bt)
                    for i1l in range(len(keys1)):
                        term = jnp.where(rk < nb[i1l], bt, zero) * ab[i1l]
                        gates[i1l][r] = term if h == 0 else gates[i1l][r] + term
            for i1l in range(len(keys1)):
                for r in range(n_tiles):
                    row0 = i1l * PK_KEYS + r * BF16_ROWS
                    hh = hk[row0:row0 + BF16_ROWS, lanes]
                    act = hh * (1.0 + lax.erf(hh * (2.0 ** -0.5)))
                    w0 = (k * sub + row0) // 2
                    w_scr[w0:w0 + 8, lanes] = pltpu.bitcast(gates[i1l][r] * act.astype(BF16), jnp.uint32)
    w_all = pltpu.bitcast(w_scr[...], BF16)
    acc_scr[...] += jnp.dot(pltpu.bitcast(vt_ref[...], BF16), w_all, preferred_element_type=F32)

    @pl.when(j == nj - 1)
    def _():
        y = alpha * x_ref[...] + acc_scr[...].T
        out_ref[...] = _layer_norm(y, g_ref[...], beta_ref[...])


def _peer_dense(x2d, u, vt, rk, b, n, a, g, beta, alpha, tt=512, ec=2048, sub=256):
    T, D = x2d.shape
    n_exp = 2 * u.shape[0]
    keys_per_chunk = ec // PK_KEYS
    tok = pl.BlockSpec((PK_HEADS, PK_KEYS // 2, tt), lambda i, j: (0, 0, i))
    key = pl.BlockSpec((PK_HEADS, keys_per_chunk, tt), lambda i, j: (0, j, i))
    fixed = pl.BlockSpec((1, D), lambda i, j: (0, 0))
    return pl.pallas_call(
        functools.partial(_peer_dense_kernel, alpha=alpha, sub=sub),
        grid=(T // tt, n_exp // ec),
        in_specs=[pl.BlockSpec((tt, D), lambda i, j: (i, 0)),
                  pl.BlockSpec((ec // 2, D), lambda i, j: (j, 0)),
                  pl.BlockSpec((D // 2, ec), lambda i, j: (0, j)),
                  tok, tok, key, key, fixed, fixed],
        out_specs=pl.BlockSpec((tt, D), lambda i, j: (i, 0)),
        out_shape=jax.ShapeDtypeStruct((T, D), F32),
        scratch_shapes=[pltpu.VMEM((D, tt), BF16),
                        pltpu.VMEM((ec // 2, tt), jnp.uint32),
                        pltpu.VMEM((D, tt), F32)],
        compiler_params=_cparams(2),
        name="peer_dense",
    )(x2d, u, vt, rk, b, n, a, g, beta)


def _rope_tables(seq):
    half = HEAD_DIM // 2
    inv = 1.0 / (ROPE_THETA ** (jnp.arange(0, HEAD_DIM, 2, dtype=F32) / HEAD_DIM))
    ang = jnp.arange(seq, dtype=F32)[:, None] * inv[None, :]
    cos = jnp.tile(jnp.cos(ang), (1, LANES // half))
    sin = jnp.tile(jnp.sin(ang), (1, LANES // half))
    first_half = (jnp.arange(LANES) % HEAD_DIM) < half
    return (cos, jnp.where(first_half[None, :], -sin, 0.0), jnp.where(first_half[None, :], 0.0, sin))


def _row_pair_kernel(x_ref, o_ref, *, transpose):
    x = x_ref[0]
    if transpose:
        x = x.T
    o_ref[...] = pltpu.bitcast(x.astype(BF16), jnp.uint32)


def _row_pair_words(tables, layer, transpose=False, blk=1024):
    _, rows, cols = tables.shape
    if transpose:
        out_shape, out_spec = (cols // 2, rows), pl.BlockSpec((cols // 2, blk), lambda i: (0, i))
    else:
        out_shape, out_spec = (rows // 2, cols), pl.BlockSpec((blk // 2, cols), lambda i: (i, 0))
    return pl.pallas_call(
        functools.partial(_row_pair_kernel, transpose=transpose),
        grid=(rows // blk,),
        in_specs=[pl.BlockSpec((1, blk, cols), lambda i: (layer, i, 0))],
        out_specs=out_spec,
        out_shape=jax.ShapeDtypeStruct(out_shape, jnp.uint32),
        compiler_params=_cparams(1),
        name="row_pair_words",
    )(tables)


def _swap_halves(w):
    n = w.shape[-1] // 2
    return jnp.concatenate([w[..., n:], w[..., :n]], axis=-1)


def kernel(x, da_w_qkv, da_lambda, da_subln_g, da_w_o, sw_w_qkv, sw_b_qkv, sw_sinks, sw_w_o, sw_b_o,
           pk_w_query, pk_sub_keys, pk_u, pk_v, ln1_g, ln1_b, ln2_g, ln2_b):
    batch, seq, d_model = x.shape
    depth = pk_w_query.shape[0]
    T = batch * seq
    alpha = (2 * depth) ** 0.25
    tables = _rope_tables(seq)
    da_heads = d_model // (2 * HEAD_DIM)
    sw_heads = d_model // HEAD_DIM
    sw_kv = (sw_w_qkv.shape[-1] - d_model) // (2 * HEAD_DIM)
    assert sw_kv == 2 and sw_kv * HEAD_DIM == LANES
    row = lambda v: v.astype(F32).reshape(1, -1)

    x2d = x.reshape(T, d_model).astype(F32)
    for i in range(depth):
        j = i // 2
        if i % 2 == 0:
            lambda_init = 0.8 - 0.6 * math.exp(-0.3 * i)
            w = da_w_qkv[j].astype(BF16)
            qk, vt = _qkv_rope(x2d, w, jnp.zeros((1, w.shape[1]), F32), tables, seq,
                               n_q_groups=da_heads, n_rope_groups=2 * da_heads, n_v_groups=da_heads,
                               q_scale=HEAD_DIM ** -0.5 * math.log2(math.e))
            mix_in = _diff_attention(qk, vt, da_lambda[j], da_subln_g[j], batch, seq, da_heads, lambda_init)
            w_o, b_o = da_w_o[j].astype(BF16), jnp.zeros((1, d_model), F32)
        else:
            wq_, wk_, wv_ = (sw_w_qkv[j][:, :d_model], sw_w_qkv[j][:, d_model:d_model + LANES],
                             sw_w_qkv[j][:, d_model + LANES:])
            bq_, bk_, bv_ = (sw_b_qkv[j][:d_model], sw_b_qkv[j][d_model:d_model + LANES],
                             sw_b_qkv[j][d_model + LANES:])
            w = jnp.concatenate([wq_, wk_, _swap_halves(wk_), wv_, _swap_halves(wv_)], axis=1).astype(BF16)
            bias = jnp.concatenate([bq_, bk_, _swap_halves(bk_), bv_, _swap_halves(bv_)]).astype(F32)
            n_qg = d_model // LANES
            qk, vt = _qkv_rope(x2d, w, bias.reshape(1, -1), tables, seq,
                               n_q_groups=n_qg, n_rope_groups=n_qg + 2, n_v_groups=2,
                               q_scale=HEAD_DIM ** -0.5 * math.log2(math.e))
            mix_in = _swa_attention(qk, vt, sw_sinks[j], batch, seq, sw_heads, sw_kv)
            w_o, b_o = sw_w_o[j].astype(BF16), row(sw_b_o[j])
        x2d = _proj_ln(x2d, mix_in, w_o, b_o, row(ln1_g[i]), row(ln1_b[i]), alpha)

        keys = pk_sub_keys[i].astype(BF16)
        zk = jnp.zeros_like(keys[:, 0])
        keys_padded = jnp.stack([jnp.concatenate([keys[:, 0], zk], axis=-1),
                                 jnp.concatenate([zk, keys[:, 1]], axis=-1)], axis=1)
        rk, b, n, a = _peer_route(x2d, pk_w_query[i].astype(BF16), keys_padded)
        x2d = _peer_dense(x2d, _row_pair_words(pk_u.astype(F32), i),
                          _row_pair_words(pk_v.astype(F32), i, transpose=True), rk, b, n, a,
                          row(ln2_g[i]), row(ln2_b[i]), alpha)
    return x2d.reshape(batch, seq, d_model).astype(x.dtype)
```

```python
import functools
import math

import jax
import jax.numpy as jnp
from jax import lax
from jax.experimental import pallas as pl
from jax.experimental.pallas import tpu as pltpu

F32 = jnp.float32
BF16 = jnp.bfloat16

LANES = 128
HEAD_DIM = 64
ROPE_THETA = 10000.0
LN_EPS = 1e-5
NEG = -1e30
BIG = 1e30

PK_HEADS = 8
PK_KEYS = 128
PK_TOPK = 16
SW_WINDOW = 128
VMEM_LIMIT = 48 * 1024 * 1024

_NT = (((1,), (1,)), ((), ()))


def _cparams(n_axes, flags=None):
    return pltpu.CompilerParams(dimension_semantics=("arbitrary",) * n_axes,
                                vmem_limit_bytes=VMEM_LIMIT, flags=flags)


def _qkv_kernel(x_ref, w_ref, b_ref, c_ref, s1_ref, s2_ref, o_ref, vt_ref, *,
                n_groups, n_q_groups, n_rope_groups, chunk, q_scale):
    xb = x_ref[...].astype(BF16)
    cos = c_ref[...]
    sin_lo = s1_ref[...]
    sin_hi = s2_ref[...]
    n_main = n_groups - vt_ref.shape[1]
    for c0 in range(0, n_groups, chunk):
        ng = min(chunk, n_groups - c0)
        cols = slice(c0 * LANES, (c0 + ng) * LANES)
        acc = jnp.dot(xb, w_ref[:, cols], preferred_element_type=F32) + b_ref[:, cols]
        for g in range(ng):
            gg = c0 + g
            a = acc[:, g * LANES:(g + 1) * LANES]
            if gg < n_rope_groups:
                a = a * cos + pltpu.roll(a, 96, 1) * sin_lo + pltpu.roll(a, 32, 1) * sin_hi
            if gg < n_q_groups:
                a = a * q_scale
            if gg < n_main:
                o_ref[:, gg * LANES:(gg + 1) * LANES] = a.astype(BF16)
            else:
                vt_ref[0, gg - n_main, 0] = a.T.astype(BF16)


def _qkv_rope(x2d, w, bias, tables, seq, n_q_groups, n_rope_groups, n_v_groups, q_scale, tm=512):
    T, D = x2d.shape
    N = w.shape[1]
    n_groups = N // LANES
    n_main = n_groups - n_v_groups
    pos_blocks = seq // tm
    kern = functools.partial(_qkv_kernel, n_groups=n_groups, n_q_groups=n_q_groups,
                             n_rope_groups=n_rope_groups, chunk=4, q_scale=q_scale)
    tab_spec = pl.BlockSpec((tm, LANES), lambda i: (i % pos_blocks, 0))
    return pl.pallas_call(
        kern,
        grid=(T // tm,),
        in_specs=[pl.BlockSpec((tm, D), lambda i: (i, 0)),
                  pl.BlockSpec((D, N), lambda i: (0, 0)),
                  pl.BlockSpec((1, N), lambda i: (0, 0)),
                  tab_spec, tab_spec, tab_spec],
        out_specs=[pl.BlockSpec((tm, n_main * LANES), lambda i: (i, 0)),
                   pl.BlockSpec((1, n_v_groups, 1, LANES, tm),
                                lambda i: (i // pos_blocks, 0, i % pos_blocks, 0, 0))],
        out_shape=[jax.ShapeDtypeStruct((T, n_main * LANES), BF16),
                   jax.ShapeDtypeStruct((T // seq, n_v_groups, pos_blocks, LANES, tm), BF16)],
        compiler_params=_cparams(1),
        name="qkv_rope",
    )(x2d, w, bias, *tables)


def _diff_attn_kernel(lam_ref, g_ref, q_ref, qn_ref, k_ref, vt_ref, o_ref, s_scr, *, tq, tk, lambda_init):
    qi = pl.program_id(2)
    lane = lax.broadcasted_iota(jnp.int32, (tq, LANES), 1)

    def stack_maps(q):
        zero = jnp.zeros_like(q)
        return jnp.concatenate([jnp.where(lane < HEAD_DIM, q, zero),
                                jnp.where(lane >= HEAD_DIM, q, zero)], axis=0)

    qs = stack_maps(q_ref[0])
    nq = 2 * tq

    def put_scores(slot, j, queries=None):
        kb = k_ref[0, pl.ds(pl.multiple_of(j * tk, tk), tk), :]
        s_scr[slot] = lax.dot_general(kb, qs if queries is None else queries, _NT,
                                      preferred_element_type=F32)

    def update(slot, j, carry, masked):
        m, l, acc = carry
        s = s_scr[slot]
        if masked:
            key = j * tk + lax.broadcasted_iota(jnp.int32, (tk, nq), 0)
            col = lax.broadcasted_iota(jnp.int32, (tk, nq), 1)
            qpos = qi * tq + jnp.where(col >= tq, col - tq, col)
            s = jnp.where(key <= qpos, s, NEG)
        m_new = jnp.maximum(m, jnp.max(s, axis=0, keepdims=True))
        a = jnp.exp2(m - m_new)
        p = jnp.exp2(s - m_new)
        l = a * l + jnp.sum(p, axis=0, keepdims=True)
        acc = a * acc + jnp.dot(vt_ref[0, 0, j], p.astype(BF16), preferred_element_type=F32)
        return m_new, l, acc

    def pair(i, carry):
        j = 2 * i
        put_scores(1, j + 1)
        carry = update(0, j, carry, False)
        put_scores(0, j + 2)
        return update(1, j + 1, carry, False)

    def tail_even(carry):
        return update(0, n_full, carry, True)

    def tail_odd(carry):
        put_scores(1, n_full)
        return update(1, n_full, update(0, n_full - 1, carry, False), True)

    init = (jnp.full((1, nq), NEG, F32), jnp.zeros((1, nq), F32), jnp.zeros((LANES, nq), F32))
    n_full = (qi * tq) // tk

    @pl.when(qi == 0)
    def _():
        put_scores(0, 0)

    carry = lax.fori_loop(0, n_full // 2, pair, init)
    m, l, acc = lax.cond(n_full % 2 == 1, tail_odd, tail_even, carry)
    put_scores(0, 0, stack_maps(qn_ref[0]))

    lp = lam_ref[...]
    lam = (jnp.exp(jnp.sum(lp[0:1] * lp[1:2], axis=1, keepdims=True))
           - jnp.exp(jnp.sum(lp[2:3] * lp[3:4], axis=1, keepdims=True)) + lambda_init)
    o = acc / l
    od = o[:, :tq] - lam * o[:, tq:]
    ms = jnp.mean(od * od, axis=0, keepdims=True)
    y = od * lax.rsqrt(ms + LN_EPS) * g_ref[...] * (1.0 - lambda_init)
    o_ref[0] = y.T.astype(BF16)


def _diff_attention(qk, vt, lam_params, subln_g, batch, seq, n_heads, lambda_init, tq=512):
    tk = vt.shape[-1]
    assert tq == tk
    T, N = qk.shape
    d_model = n_heads * LANES
    qkv3 = qk.reshape(batch, seq, N)
    kern = functools.partial(_diff_attn_kernel, tq=tq, tk=tk, lambda_init=lambda_init)
    out = pl.pallas_call(
        kern,
        grid=(batch, n_heads, seq // tq),
        in_specs=[pl.BlockSpec((4, HEAD_DIM), lambda b, h, i: (0, 0)),
                  pl.BlockSpec((LANES, 1), lambda b, h, i: (0, 0)),
                  pl.BlockSpec((1, tq, LANES), lambda b, h, i: (b, i, h)),
                  pl.BlockSpec((1, tq, LANES), lambda b, h, i: (b, jnp.minimum(i + 1, seq // tq - 1), h)),
                  pl.BlockSpec((1, seq, LANES), lambda b, h, i: (b, 0, n_heads + h)),
                  pl.BlockSpec((1, 1, seq // tk, LANES, tk), lambda b, h, i: (b, h, 0, 0, 0))],
        out_specs=pl.BlockSpec((1, tq, LANES), lambda b, h, i: (b, i, h)),
        out_shape=jax.ShapeDtypeStruct((batch, seq, d_model), BF16),
        scratch_shapes=[pltpu.VMEM((2, tk, 2 * tq), F32)],
        compiler_params=_cparams(3),
        name="diff_attention",
    )(lam_params.astype(F32), subln_g.astype(F32).reshape(LANES, 1), qkv3, qkv3, qkv3, vt)
    return out.reshape(T, d_model)


def _swa_kernel(sink_ref, q_ref, kc_ref, kp_ref, ksc_ref, ksp_ref,
                vtc_ref, vtp_ref, vtsc_ref, vtsp_ref, o_ref, *, blk, n_kv, group):
    n = pl.program_id(1)
    pairs = group // 2
    nq = pairs * blk
    lo_lane = lax.broadcasted_iota(jnp.int32, (blk, LANES), 1) < HEAD_DIM
    lo_feat = lax.broadcasted_iota(jnp.int32, (LANES, blk), 0) < HEAD_DIM
    k_nat = jnp.concatenate([kp_ref[0], kc_ref[0]], axis=0)
    k_swp = jnp.concatenate([ksp_ref[0], ksc_ref[0]], axis=0)
    vt_nat = jnp.concatenate([vtp_ref[0, 0, 0], vtc_ref[0, 0, 0]], axis=1)
    vt_swp = jnp.concatenate([vtsp_ref[0, 0, 0], vtsc_ref[0, 0, 0]], axis=1)
    key = lax.broadcasted_iota(jnp.int32, (2 * blk, nq), 0)
    col = lax.broadcasted_iota(jnp.int32, (2 * blk, nq), 1)
    rel = (col & (blk - 1)) + blk - key
    first_key = jnp.where(n > 0, 0, blk)
    valid = (rel >= 0) & (rel < SW_WINDOW) & (key >= first_key)
    for kv in range(n_kv):
        o_par = []
        for parity in range(2):
            use_nat = (kv == 0) == (parity == 0)
            kx = k_nat if use_nat else k_swp
            vtx = vt_nat if use_nat else vt_swp
            qm, sk = [], []
            for p in range(pairs):
                gcol = (kv * pairs + p) * LANES
                qg = q_ref[0, :, gcol:gcol + LANES]
                keep = lo_lane if parity == 0 else ~lo_lane
                qm.append(jnp.where(keep, qg, jnp.zeros_like(qg)))
                head = kv * group + 2 * p + parity
                sk.append(sink_ref[head:head + 1, :])
            qst = jnp.concatenate(qm, axis=0)
            sink = jnp.concatenate(sk, axis=1)
            s = lax.dot_general(kx, qst, _NT, preferred_element_type=F32)
            s = jnp.where(valid, s, NEG)
            m = jnp.maximum(jnp.max(s, axis=0, keepdims=True), sink)
            e = jnp.exp2(s - m)
            den = jnp.sum(e, axis=0, keepdims=True) + jnp.exp2(sink - m)
            prob = (e * (1.0 / den)).astype(BF16)
            o_par.append(jnp.dot(vtx, prob, preferred_element_type=F32))
        for p in range(pairs):
            gcol = (kv * pairs + p) * LANES
            cols = slice(p * blk, (p + 1) * blk)
            ot = jnp.where(lo_feat, o_par[0][:, cols], o_par[1][:, cols])
            o_ref[0, :, gcol:gcol + LANES] = ot.T.astype(BF16)


def _swa_attention(qkv, vt, sinks, batch, seq, n_heads, n_kv, blk=128):
    T, N = qkv.shape
    d_model = n_heads * HEAD_DIM
    qg = d_model // LANES
    qkv3 = qkv.reshape(batch, seq, N)
    nb = seq // blk
    per_tile = vt.shape[-1] // blk
    sink_b = jnp.broadcast_to((sinks.astype(F32) * math.log2(math.e))[:, None], (n_heads, LANES))

    def cur(col):
        return pl.BlockSpec((1, blk, LANES), lambda b, n: (b, n, col))

    def prev(col):
        return pl.BlockSpec((1, blk, LANES), lambda b, n: (b, jnp.maximum(n - 1, 0), col))

    def vt_spec(g, back):
        def index(b, n):
            m = jnp.maximum(n - back, 0)
            return (b, g, m // per_tile, 0, m % per_tile)
        return pl.BlockSpec((1, 1, 1, LANES, blk), index)

    kern = functools.partial(_swa_kernel, blk=blk, n_kv=n_kv, group=n_heads // n_kv)
    out = pl.pallas_call(
        kern,
        grid=(batch, nb),
        in_specs=[pl.BlockSpec((n_heads, LANES), lambda b, n: (0, 0)),
                  pl.BlockSpec((1, blk, d_model), lambda b, n: (b, n, 0)),
                  cur(qg), prev(qg), cur(qg + 1), prev(qg + 1),
                  vt_spec(0, 0), vt_spec(0, 1), vt_spec(1, 0), vt_spec(1, 1)],
        out_specs=pl.BlockSpec((1, blk, d_model), lambda b, n: (b, n, 0)),
        out_shape=jax.ShapeDtypeStruct((batch, seq, d_model), BF16),
        compiler_params=_cparams(2),
        name="swa_attention",
    )(sink_b, qkv3, qkv3, qkv3, qkv3, qkv3, vt, vt, vt, vt)
    return out.reshape(T, d_model)


def _layer_norm(y, g, b):
    mu = jnp.mean(y, axis=-1, keepdims=True)
    yc = y - mu
    var = jnp.mean(yc * yc, axis=-1, keepdims=True)
    return yc * lax.rsqrt(var + LN_EPS) * g + b


def _proj_ln_kernel(x_ref, o_ref, w_ref, b_ref, g_ref, beta_ref, out_ref, *, alpha):
    mix = jnp.dot(o_ref[...], w_ref[...], preferred_element_type=F32) + b_ref[...]
    out_ref[...] = _layer_norm(alpha * x_ref[...] + mix, g_ref[...], beta_ref[...])


def _proj_ln(x2d, o2d, w, bias, g, beta, alpha, tm=512):
    T, D = x2d.shape
    K = o2d.shape[1]
    row = lambda i: (i, 0)
    fixed = lambda i: (0, 0)
    return pl.pallas_call(
        functools.partial(_proj_ln_kernel, alpha=alpha),
        grid=(T // tm,),
        in_specs=[pl.BlockSpec((tm, D), row), pl.BlockSpec((tm, K), row),
                  pl.BlockSpec((K, D), fixed), pl.BlockSpec((1, D), fixed),
                  pl.BlockSpec((1, D), fixed), pl.BlockSpec((1, D), fixed)],
        out_specs=pl.BlockSpec((tm, D), row),
        out_shape=jax.ShapeDtypeStruct((T, D), F32),
        compiler_params=_cparams(1),
        name="proj_ln",
    )(x2d, o2d, w, bias, g, beta)


def _tree(fn, xs):
    xs = list(xs)
    while len(xs) > 1:
        nxt = [fn(xs[i], xs[i + 1]) for i in range(0, len(xs) - 1, 2)]
        if len(xs) % 2:
            nxt.append(xs[-1])
        xs = nxt
    return xs[0]


def _pair_words(v):
    bits = pltpu.bitcast(v.astype(BF16).astype(F32), jnp.uint32)
    return bits | (bits >> 16)


def _sort_network(n):
    pairs = []
    p = 1
    while p < n:
        k = p
        while k >= 1:
            for j in range(k % p, n - k, 2 * k):
                for i in range(min(k, n - j - k)):
                    if (i + j) // (2 * p) == (i + j + k) // (2 * p):
                        pairs.append((i + j, i + j + k))
            k //= 2
        p *= 2
    return tuple(pairs)


_SORT_TOPK = _sort_network(PK_TOPK)
SUBLANES = 8


def _exchange(x, i, j):
    x[i], x[j] = jnp.maximum(x[i], x[j]), jnp.minimum(x[i], x[j])


def _top16_sorted(x):
    x = list(x)
    for i, j in _SORT_TOPK:
        _exchange(x, i, j)
    for shift in (4, 2, 1):
        y = [pltpu.roll(v, shift, 0) for v in x]
        x = [jnp.maximum(x[i], y[PK_TOPK - 1 - i]) for i in range(PK_TOPK)]
        d = PK_TOPK // 2
        while d >= 1:
            for i in range(PK_TOPK):
                if not i & d:
                    _exchange(x, i, i + d)
            d //= 2
    return x


def _leading_count(rows, hit):
    pick = lambda c, a, b: jnp.where(c, a, b)
    b3 = hit(rows[7])
    b2 = hit(pick(b3, rows[11], rows[3]))
    b1 = hit(pick(b3, pick(b2, rows[13], rows[9]), pick(b2, rows[5], rows[1])))
    b0 = hit(pick(b3, pick(b2, pick(b1, rows[14], rows[12]), pick(b1, rows[10], rows[8])),
                  pick(b2, pick(b1, rows[6], rows[4]), pick(b1, rows[2], rows[0]))))
    count = (jnp.where(b3, 8.0, 0.0) + jnp.where(b2, 4.0, 0.0)
             + jnp.where(b1, 2.0, 0.0) + jnp.where(b0, 1.0, 0.0))
    return jnp.where(hit(rows[PK_TOPK - 1]), float(PK_TOPK), count)


def _peer_route_kernel(x_ref, wq_ref, kp_ref, rk_o, b_o, n_o, a_o,
                       q_scr, sc_scr, top_scr, st_scr, sg_scr):
    tm = x_ref.shape[0]
    q = jnp.dot(x_ref[...].astype(BF16), wq_ref[...], preferred_element_type=F32)
    for h in range(PK_HEADS):
        q_scr[h] = q[:, h * LANES:(h + 1) * LANES].astype(BF16)

    def scores_and_tops(h, carry):
        qh = q_scr[h]
        for c in range(2):
            sc = lax.dot_general(kp_ref[h, c], qh, _NT, preferred_element_type=F32)
            sc_scr[h, c] = sc
            tops = [[] for _ in range(PK_TOPK)]
            for lg in range(tm // LANES):
                slabs = [sc[SUBLANES * k:SUBLANES * (k + 1), lg * LANES:(lg + 1) * LANES]
                         for k in range(PK_KEYS // SUBLANES)]
                for k, v in enumerate(_top16_sorted(slabs)):
                    tops[k].append(v[0:1, :])
            for k in range(PK_TOPK):
                top_scr[c, k, pl.ds(h, 1), :] = jnp.concatenate(tops[k], axis=1)
        return carry

    lax.fori_loop(0, PK_HEADS, scores_and_tops, 0)

    t1 = [top_scr[0, k] for k in range(PK_TOPK)]
    t2 = [top_scr[1, k] for k in range(PK_TOPK)]
    cand = {(r, j): t1[r] + t2[j] for r in range(PK_TOPK) for j in range(PK_TOPK)
            if (r + 1) * (j + 1) <= PK_TOPK}
    cur = list(cand.values())
    thr = None
    for k in range(PK_TOPK):
        thr = _tree(jnp.maximum, cur)
        if k + 1 < PK_TOPK:
            cur = [jnp.where(c == thr, NEG, c) for c in cur]
    cmax = cand[(0, 0)]
    z = _tree(jnp.add, [jnp.where(c >= thr, jnp.exp(c - cmax), 0.0) for c in cand.values()])
    st_scr[0] = 0.5 / z
    for j in range(PK_TOPK):
        sg = jnp.full(thr.shape, BIG, F32)
        for r in range(PK_TOPK):
            if (r, j) in cand:
                sg = jnp.where(cand[(r, j)] >= thr, t1[r], sg)
        sg_scr[j] = sg

    def per_head(h, carry):
        s1 = sc_scr[h, 0]
        s2 = sc_scr[h, 1]
        rank = _leading_count([top_scr[1, k, pl.ds(h, 1), :] for k in range(PK_TOPK)],
                              lambda t: t > s2)
        n = _leading_count([sg_scr[k, pl.ds(h, 1), :] for k in range(PK_TOPK)],
                           lambda sg: s1 >= sg)
        rk_o[h] = pltpu.bitcast(rank.astype(BF16), jnp.uint32)
        n_o[h] = _pair_words(n)
        a_o[h] = _pair_words(jnp.exp(s1 - top_scr[0, 0, pl.ds(h, 1), :]))
        b = jnp.exp(s2 - top_scr[1, 0, pl.ds(h, 1), :]) * st_scr[0, pl.ds(h, 1), :]
        b_o[h] = pltpu.bitcast(b.astype(BF16), jnp.uint32)
        return carry

    lax.fori_loop(0, PK_HEADS, per_head, 0)


def _peer_route(x2d, wq, keys_padded, tm=256):
    T, D = x2d.shape
    half = jax.ShapeDtypeStruct((PK_HEADS, PK_KEYS // 2, T), jnp.uint32)
    word = jax.ShapeDtypeStruct((PK_HEADS, PK_KEYS, T), jnp.uint32)
    hspec = pl.BlockSpec((PK_HEADS, PK_KEYS // 2, tm), lambda i: (0, 0, i))
    ospec = pl.BlockSpec((PK_HEADS, PK_KEYS, tm), lambda i: (0, 0, i))
    return pl.pallas_call(
        _peer_route_kernel,
        grid=(T // tm,),
        in_specs=[pl.BlockSpec((tm, D), lambda i: (i, 0)),
                  pl.BlockSpec(wq.shape, lambda i: (0, 0)),
                  pl.BlockSpec(keys_padded.shape, lambda i: (0, 0, 0, 0))],
        out_specs=[hspec, hspec, ospec, ospec],
        out_shape=[half, half, word, word],
        scratch_shapes=[pltpu.VMEM((PK_HEADS, tm, LANES), BF16),
                        pltpu.VMEM((PK_HEADS, 2, PK_KEYS, tm), F32),
                        pltpu.VMEM((2, PK_TOPK, PK_HEADS, tm), F32),
                        pltpu.VMEM((1, PK_HEADS, tm), F32),
                        pltpu.VMEM((PK_TOPK, PK_HEADS, tm), F32)],
        compiler_params=_cparams(1),
        name="peer_route",
    )(x2d, wq, keys_padded)


BF16_ROWS = 16


def _peer_dense_kernel(x_ref, u_ref, vt_ref, rk_ref, b_ref, n_ref, a_ref, g_ref, beta_ref,
                       out_ref, xb_scr, w_scr, acc_scr, *, alpha, sub):
    j = pl.program_id(1)
    nj = pl.num_programs(1)
    ec, tt = 2 * w_scr.shape[0], w_scr.shape[1]

    @pl.when(j == 0)
    def _():
        xb_scr[...] = x_ref[...].T.astype(BF16)
        acc_scr[...] = jnp.zeros_like(acc_scr)

    xt = xb_scr[...]
    for k in range(ec // sub):
        uk = pltpu.bitcast(u_ref[k * sub // 2:(k + 1) * sub // 2, :], BF16)
        hk = jnp.dot(uk, xt, preferred_element_type=F32)
        keys1 = range(k * (sub // PK_KEYS), (k + 1) * (sub // PK_KEYS))
        for lg in range(tt // LANES):
            lanes = slice(lg * LANES, (lg + 1) * LANES)
            splat = lambda ref, h, i1: pltpu.bitcast(
                jnp.broadcast_to(ref[h, i1:i1 + 1, lanes], (8, LANES)), BF16)
            n_tiles = PK_KEYS // BF16_ROWS
            gates = [[None] * n_tiles for _ in keys1]
            for h in range(PK_HEADS):
                nb = [splat(n_ref, h, i1) for i1 in keys1]
                ab = [splat(a_ref, h, i1) for i1 in keys1]
                for r in range(n_tiles):
                    words = slice(r * 8, (r + 1) * 8)
                    bt = pltpu.bitcast(b_ref[h, words, lanes], BF16)
                    rk = pltpu.bitcast(rk_ref[h, words, lanes], BF16)
                    zero = jnp.zeros_like(bt)
                    for i1l in range(len(keys1)):
                        term = jnp.where(rk < nb[i1l], bt, zero) * ab[i1l]
                        gates[i1l][r] = term if h == 0 else gates[i1l][r] + term
            for i1l in range(len(keys1)):
                for r in range(n_tiles):
                    row0 = i1l * PK_KEYS + r * BF16_ROWS
                    hh = hk[row0:row0 + BF16_ROWS, lanes]
                    act = hh * (1.0 + lax.erf(hh * (2.0 ** -0.5)))
                    w0 = (k * sub + row0) // 2
                    w_scr[w0:w0 + 8, lanes] = pltpu.bitcast(gates[i1l][r] * act.astype(BF16), jnp.uint32)
    w_all = pltpu.bitcast(w_scr[...], BF16)
    acc_scr[...] += jnp.dot(pltpu.bitcast(vt_ref[...], BF16), w_all, preferred_element_type=F32)

    @pl.when(j == nj - 1)
    def _():
        y = alpha * x_ref[...] + acc_scr[...].T
        out_ref[...] = _layer_norm(y, g_ref[...], beta_ref[...])


def _peer_dense(x2d, u, vt, rk, b, n, a, g, beta, alpha, tt=512, ec=2048, sub=256):
    T, D = x2d.shape
    n_exp = 2 * u.shape[0]
    keys_per_chunk = ec // PK_KEYS
    tok = pl.BlockSpec((PK_HEADS, PK_KEYS // 2, tt), lambda i, j: (0, 0, i))
    key = pl.BlockSpec((PK_HEADS, keys_per_chunk, tt), lambda i, j: (0, j, i))
    fixed = pl.BlockSpec((1, D), lambda i, j: (0, 0))
    return pl.pallas_call(
        functools.partial(_peer_dense_kernel, alpha=alpha, sub=sub),
        grid=(T // tt, n_exp // ec),
        in_specs=[pl.BlockSpec((tt, D), lambda i, j: (i, 0)),
                  pl.BlockSpec((ec // 2, D), lambda i, j: (j, 0)),
                  pl.BlockSpec((D // 2, ec), lambda i, j: (0, j)),
                  tok, tok, key, key, fixed, fixed],
        out_specs=pl.BlockSpec((tt, D), lambda i, j: (i, 0)),
        out_shape=jax.ShapeDtypeStruct((T, D), F32),
        scratch_shapes=[pltpu.VMEM((D, tt), BF16),
                        pltpu.VMEM((ec // 2, tt), jnp.uint32),
                        pltpu.VMEM((D, tt), F32)],
        compiler_params=_cparams(2),
        name="peer_dense",
    )(x2d, u, vt, rk, b, n, a, g, beta)


def _rope_tables(seq):
    half = HEAD_DIM // 2
    inv = 1.0 / (ROPE_THETA ** (jnp.arange(0, HEAD_DIM, 2, dtype=F32) / HEAD_DIM))
    ang = jnp.arange(seq, dtype=F32)[:, None] * inv[None, :]
    cos = jnp.tile(jnp.cos(ang), (1, LANES // half))
    sin = jnp.tile(jnp.sin(ang), (1, LANES // half))
    first_half = (jnp.arange(LANES) % HEAD_DIM) < half
    return (cos, jnp.where(first_half[None, :], -sin, 0.0), jnp.where(first_half[None, :], 0.0, sin))


def _row_pair_kernel(x_ref, o_ref, *, transpose):
    x = x_ref[0]
    if transpose:
        x = x.T
    o_ref[...] = pltpu.bitcast(x.astype(BF16), jnp.uint32)


def _row_pair_words(tables, layer, transpose=False, blk=1024):
    _, rows, cols = tables.shape
    if transpose:
        out_shape, out_spec = (cols // 2, rows), pl.BlockSpec((cols // 2, blk), lambda i: (0, i))
    else:
        out_shape, out_spec = (rows // 2, cols), pl.BlockSpec((blk // 2, cols), lambda i: (i, 0))
    return pl.pallas_call(
        functools.partial(_row_pair_kernel, transpose=transpose),
        grid=(rows // blk,),
        in_specs=[pl.BlockSpec((1, blk, cols), lambda i: (layer, i, 0))],
        out_specs=out_spec,
        out_shape=jax.ShapeDtypeStruct(out_shape, jnp.uint32),
        compiler_params=_cparams(1),
        name="row_pair_words",
    )(tables)


def _swap_halves(w):
    n = w.shape[-1] // 2
    return jnp.concatenate([w[..., n:], w[..., :n]], axis=-1)


def kernel(x, da_w_qkv, da_lambda, da_subln_g, da_w_o, sw_w_qkv, sw_b_qkv, sw_sinks, sw_w_o, sw_b_o,
           pk_w_query, pk_sub_keys, pk_u, pk_v, ln1_g, ln1_b, ln2_g, ln2_b):
    batch, seq, d_model = x.shape
    depth = pk_w_query.shape[0]
    T = batch * seq
    alpha = (2 * depth) ** 0.25
    tables = _rope_tables(seq)
    da_heads = d_model // (2 * HEAD_DIM)
    sw_heads = d_model // HEAD_DIM
    sw_kv = (sw_w_qkv.shape[-1] - d_model) // (2 * HEAD_DIM)
    assert sw_kv == 2 and sw_kv * HEAD_DIM == LANES
    row = lambda v: v.astype(F32).reshape(1, -1)

    x2d = x.reshape(T, d_model).astype(F32)
    for i in range(depth):
        j = i // 2
        if i % 2 == 0:
            lambda_init = 0.8 - 0.6 * math.exp(-0.3 * i)
            w = da_w_qkv[j].astype(BF16)
            qk, vt = _qkv_rope(x2d, w, jnp.zeros((1, w.shape[1]), F32), tables, seq,
                               n_q_groups=da_heads, n_rope_groups=2 * da_heads, n_v_groups=da_heads,
                               q_scale=HEAD_DIM ** -0.5 * math.log2(math.e))
            mix_in = _diff_attention(qk, vt, da_lambda[j], da_subln_g[j], batch, seq, da_heads, lambda_init)
            w_o, b_o = da_w_o[j].astype(BF16), jnp.zeros((1, d_model), F32)
        else:
            wq_, wk_, wv_ = (sw_w_qkv[j][:, :d_model], sw_w_qkv[j][:, d_model:d_model + LANES],
                             sw_w_qkv[j][:, d_model + LANES:])
            bq_, bk_, bv_ = (sw_b_qkv[j][:d_model], sw_b_qkv[j][d_model:d_model + LANES],
                             sw_b_qkv[j][d_model + LANES:])
            w = jnp.concatenate([wq_, wk_, _swap_halves(wk_), wv_, _swap_halves(wv_)], axis=1).astype(BF16)
            bias = jnp.concatenate([bq_, bk_, _swap_halves(bk_), bv_, _swap_halves(bv_)]).astype(F32)
            n_qg = d_model // LANES
            qk, vt = _qkv_rope(x2d, w, bias.reshape(1, -1), tables, seq,
                               n_q_groups=n_qg, n_rope_groups=n_qg + 2, n_v_groups=2,
                               q_scale=HEAD_DIM ** -0.5 * math.log2(math.e))
            mix_in = _swa_attention(qk, vt, sw_sinks[j], batch, seq, sw_heads, sw_kv)
            w_o, b_o = sw_w_o[j].astype(BF16), row(sw_b_o[j])
        x2d = _proj_ln(x2d, mix_in, w_o, b_o, row(ln1_g[i]), row(ln1_b[i]), alpha)

        keys = pk_sub_keys[i].astype(BF16)
        zk = jnp.zeros_like(keys[:, 0])
        keys_padded = jnp.stack([jnp.concatenate([keys[:, 0], zk], axis=-1),
                                 jnp.concatenate([zk, keys[:, 1]], axis=-1)], axis=1)
        rk, b, n, a = _peer_route(x2d, pk_w_query[i].astype(BF16), keys_padded)
        x2d = _peer_dense(x2d, _row_pair_words(pk_u.astype(F32), i),
                          _row_pair_words(pk_v.astype(F32), i, transpose=True), rk, b, n, a,
                          row(ln2_g[i]), row(ln2_b[i]), alpha)
    return x2d.reshape(batch, seq, d_model).astype(x.dtype)
```
